```python
import math
import jax, jax.numpy as jnp
from jax import lax
import numpy as np

D_MODEL = 2048
BATCH = 4
SEQ = 2048
DEPTH = 1

HEAD_DIM = 128
MOBA_HEADS = 8
MOBA_WIDTH = MOBA_HEADS * HEAD_DIM
MOBA_BLOCK = 256
MOBA_TOPK = 3
MOBA_Q_CHUNK = 16
DN_HEADS = 8
DN_WIDTH = DN_HEADS * HEAD_DIM
DN_CONV = 4
DN_CHUNK = 64
NORM_EPS = 1e-6
IN_SPLITS = (MOBA_WIDTH, MOBA_WIDTH, MOBA_WIDTH, MOBA_WIDTH,
             3 * DN_WIDTH, DN_WIDTH,
             DN_HEADS, DN_HEADS,
             D_MODEL, D_MODEL)
IN_WIDTH = sum(IN_SPLITS)

kernel_name = "moba_gated_deltanet_hybrid"


def rms_norm(x, gain):
    xf = x.astype(jnp.float32)
    y = xf * lax.rsqrt(jnp.mean(xf * xf, axis=-1, keepdims=True) + NORM_EPS)
    return (y * gain.astype(jnp.float32)).astype(x.dtype)


def l2_normalize(x):
    return x * lax.rsqrt(jnp.sum(x * x, axis=-1, keepdims=True) + NORM_EPS)


def alibi_slopes(n_heads):
    return jnp.exp2(-8.0 * jnp.arange(1, n_heads + 1, dtype=jnp.float32) / n_heads)


def moba_attention(q, k, v):
    b, s, nh, dh = q.shape
    q = jnp.swapaxes(q, 1, 2) * dh ** -0.5
    k = jnp.swapaxes(k, 1, 2)
    v = jnp.swapaxes(v, 1, 2)
    n_blk = -(-s // MOBA_BLOCK)
    pad = n_blk * MOBA_BLOCK - s
    k = jnp.pad(k, ((0, 0), (0, 0), (0, pad), (0, 0)))
    v = jnp.pad(v, ((0, 0), (0, 0), (0, pad), (0, 0)))
    kb = k.reshape(b, nh, n_blk, MOBA_BLOCK, dh)
    vb = v.reshape(b, nh, n_blk, MOBA_BLOCK, dh)
    slopes = alibi_slopes(nh)[None, :, None, None]
    n_top = min(MOBA_TOPK, n_blk - 1)
    n_chunk = s // MOBA_Q_CHUNK
    cq = MOBA_Q_CHUNK
    qc = q.reshape(b, nh, n_chunk, cq, dh).transpose(2, 0, 1, 3, 4)
    starts = jnp.arange(n_chunk, dtype=jnp.int32) * cq
    offs_blk = jnp.arange(MOBA_BLOCK, dtype=jnp.int32)
    bi = jnp.arange(b)[:, None, None]
    hi = jnp.arange(nh)[None, :, None]

    if n_top > 0:
        k_mean = jnp.mean(kb.astype(jnp.float32), axis=3)
        gate = jnp.einsum('bhsd,bhnd->bhsn', q.astype(jnp.float32), k_mean)
        q_blk = jnp.arange(s, dtype=jnp.int32) // MOBA_BLOCK
        past = jnp.arange(n_blk, dtype=jnp.int32)[None, :] < q_blk[:, None]
        gate = jnp.where(past, gate, -jnp.inf)
        _, top_idx = lax.top_k(gate, n_top)
        top_valid = top_idx < q_blk[:, None]
        idx_c = top_idx.reshape(b, nh, n_chunk, cq, n_top).transpose(2, 0, 1, 3, 4)
        valid_c = top_valid.reshape(b, nh, n_chunk, cq, n_top).transpose(2, 0, 1, 3, 4)
        xs = (qc, starts, idx_c, valid_c)
    else:
        xs = (qc, starts)

    def chunk_fn(args):
        qi, start = args[0], args[1]
        t = start + jnp.arange(cq, dtype=jnp.int32)
        blk = start // MOBA_BLOCK
        ko = lax.dynamic_index_in_dim(kb, blk, axis=2, keepdims=False)
        vo = lax.dynamic_index_in_dim(vb, blk, axis=2, keepdims=False)
        pos_o = blk * MOBA_BLOCK + offs_blk
        dist_o = (t[:, None] - pos_o[None, :]).astype(jnp.float32)
        lo = jnp.einsum('bhqd,bhkd->bhqk', qi, ko).astype(jnp.float32) - slopes * dist_o
        lo = jnp.where((pos_o[None, :] <= t[:, None])[None, None], lo, -jnp.inf)
        if n_top > 0:
            idx, valid = args[2], args[3]
            gidx = idx.reshape(b, nh, cq * n_top)
            kp = kb[bi, hi, gidx].reshape(b, nh, cq, n_top * MOBA_BLOCK, dh)
            vp = vb[bi, hi, gidx].reshape(b, nh, cq, n_top * MOBA_BLOCK, dh)
            pos_p = (idx[..., None] * MOBA_BLOCK + offs_blk).reshape(b, nh, cq, n_top * MOBA_BLOCK)
            dist_p = (t[None, None, :, None] - pos_p).astype(jnp.float32)
            lp = jnp.einsum('bhqd,bhqkd->bhqk', qi, kp).astype(jnp.float32) - slopes * dist_p
            lp = jnp.where(jnp.repeat(valid, MOBA_BLOCK, axis=-1), lp, -jnp.inf)
            p = jax.nn.softmax(jnp.concatenate([lp, lo], axis=-1), axis=-1).astype(vb.dtype)
            n_p = n_top * MOBA_BLOCK
            out = (jnp.einsum('bhqk,bhqkd->bhqd', p[..., :n_p], vp)
                   + jnp.einsum('bhqk,bhkd->bhqd', p[..., n_p:], vo))
        else:
            p = jax.nn.softmax(lo, axis=-1).astype(vb.dtype)
            out = jnp.einsum('bhqk,bhkd->bhqd', p, vo)
        return out

    out = lax.map(chunk_fn, xs)
    return out.transpose(1, 0, 3, 2, 4).reshape(b, s, nh * dh)


def short_conv(u, w):
    c = u.shape[-1]
    y = lax.conv_general_dilated(u, w[:, None, :].astype(u.dtype), window_strides=(1,),
                                 padding=[(DN_CONV - 1, 0)],
                                 dimension_numbers=('NWC', 'WIO', 'NWC'),
                                 feature_group_count=c)
    return jax.nn.silu(y)


def gated_delta_rule(q, k, v, g, beta):
    b, nh, s, dk = q.shape
    dv = v.shape[-1]
    c = DN_CHUNK
    n = s // c
    q = l2_normalize(q) * dk ** -0.5
    k = l2_normalize(k)
    q = q.reshape(b, nh, n, c, dk)
    k = k.reshape(b, nh, n, c, dk)
    v = v.reshape(b, nh, n, c, dv)
    g = jnp.cumsum(g.reshape(b, nh, n, c), axis=-1)
    beta = beta.reshape(b, nh, n, c)[..., None]
    kb = k * beta
    vb = v * beta
    causal = jnp.tril(jnp.ones((c, c), dtype=bool))
    strict = jnp.tril(jnp.ones((c, c), dtype=bool), -1)
    decay = jnp.exp(jnp.where(causal, g[..., :, None] - g[..., None, :], -jnp.inf))
    lower = jnp.where(strict, jnp.einsum('bhnid,bhnjd->bhnij', kb, k) * decay, 0.0)
    eye = jnp.eye(c, dtype=jnp.float32)
    tmat = lax.linalg.triangular_solve(eye + lower, jnp.broadcast_to(eye, lower.shape),
                                       left_side=True, lower=True, unit_diagonal=True)
    u = tmat @ vb
    w = tmat @ (kb * jnp.exp(g)[..., None])
    intra = jnp.where(causal, jnp.einsum('bhnid,bhnjd->bhnij', q, k) * decay, 0.0)

    def step(state, xs):
        qi, ki, ui, wi, gi, ai = xs
        v_new = ui - wi @ state
        out = (qi * jnp.exp(gi)[..., None]) @ state + ai @ v_new
        g_last = gi[..., -1:]
        state = (state * jnp.exp(g_last)[..., None]
                 + jnp.einsum('bhck,bhcv->bhkv', ki * jnp.exp(g_last - gi)[..., None], v_new))
        return state, out

    xs = tuple(jnp.moveaxis(t, 2, 0) for t in (q, k, u, w, g, intra))
    state0 = jnp.zeros((b, nh, dk, dv), jnp.float32)
    _, out = lax.scan(step, state0, xs)
    return jnp.moveaxis(out, 0, 2).reshape(b, nh, s, dv)


def hybrid_layer(x, norm_gain, w_in, conv_w, a_log, dt_bias, dn_norm_gain,
                 w_branch_a, w_branch_b, w_out):
    b, s, _ = x.shape
    f32 = jnp.float32
    h = rms_norm(x, norm_gain)
    cuts = np.cumsum(IN_SPLITS)[:-1].tolist()
    aq, ak, av, az, bqkv, bz, bbeta, balpha, gate_a, gate_b = jnp.split(h @ w_in, cuts, axis=-1)

    to_heads = lambda t: t.reshape(b, s, -1, HEAD_DIM)
    ya = moba_attention(to_heads(aq), to_heads(ak), to_heads(av)) * jax.nn.silu(az)

    bqkv = short_conv(bqkv, conv_w)
    bq, bk, bv = jnp.split(bqkv, 3, axis=-1)
    to_bhsd = lambda t: t.reshape(b, s, DN_HEADS, HEAD_DIM).transpose(0, 2, 1, 3).astype(f32)
    g = -jnp.exp(a_log.astype(f32)) * jax.nn.softplus(balpha.astype(f32) + dt_bias.astype(f32))
    beta = jax.nn.sigmoid(bbeta.astype(f32))
    o = gated_delta_rule(to_bhsd(bq), to_bhsd(bk), to_bhsd(bv),
                         jnp.swapaxes(g, 1, 2), jnp.swapaxes(beta, 1, 2))
    o = rms_norm(o.transpose(0, 2, 1, 3), dn_norm_gain).astype(x.dtype) * jax.nn.silu(to_heads(bz))
    yb = o.reshape(b, s, DN_WIDTH)

    merged = (jax.nn.sigmoid(gate_a) * (ya @ w_branch_a)
              + jax.nn.sigmoid(gate_b) * (yb @ w_branch_b))
    return x + merged @ w_out


def setup_inputs(seed: int = 0) -> dict:
    key = jax.random.key(seed)
    ks = jax.random.split(key, 12)
    f32 = jnp.float32
    nrm = lambda k, shape, scale: jax.random.normal(k, shape, f32) * scale
    x = jax.random.normal(ks[0], (BATCH, SEQ, D_MODEL), f32)
    norm_gain = 1.0 + nrm(ks[1], (DEPTH, D_MODEL), 0.02)
    w_in = nrm(ks[2], (DEPTH, D_MODEL, IN_WIDTH), D_MODEL ** -0.5)
    conv_w = nrm(ks[3], (DEPTH, DN_CONV, 3 * DN_WIDTH), DN_CONV ** -0.5)
    a_log = jnp.log(jax.random.uniform(ks[4], (DEPTH, DN_HEADS), f32, 1.0, 16.0))
    dt = jnp.exp(jax.random.uniform(ks[5], (DEPTH, DN_HEADS), f32, math.log(1e-3), math.log(1e-1)))
    dt_bias = dt + jnp.log(-jnp.expm1(-dt))
    dn_norm_gain = 1.0 + nrm(ks[6], (DEPTH, HEAD_DIM), 0.02)
    w_branch_a = nrm(ks[7], (DEPTH, MOBA_WIDTH, D_MODEL), MOBA_WIDTH ** -0.5)
    w_branch_b = nrm(ks[8], (DEPTH, DN_WIDTH, D_MODEL), DN_WIDTH ** -0.5)
    w_out = nrm(ks[9], (DEPTH, D_MODEL, D_MODEL), D_MODEL ** -0.5)
    final_norm_gain = 1.0 + nrm(ks[10], (D_MODEL,), 0.02)
    return {"x": x, "norm_gain": norm_gain, "w_in": w_in, "conv_w": conv_w,
            "a_log": a_log, "dt_bias": dt_bias, "dn_norm_gain": dn_norm_gain,
            "w_branch_a": w_branch_a, "w_branch_b": w_branch_b, "w_out": w_out,
            "final_norm_gain": final_norm_gain}


def reference(x, norm_gain, w_in, conv_w, a_log, dt_bias, dn_norm_gain,
              w_branch_a, w_branch_b, w_out, final_norm_gain):
    for layer in range(DEPTH):
        x = hybrid_layer(x, norm_gain[layer], w_in[layer], conv_w[layer], a_log[layer],
                         dt_bias[layer], dn_norm_gain[layer], w_branch_a[layer],
                         w_branch_b[layer], w_out[layer])
    return rms_norm(x, final_norm_gain)
```

```python
import functools
import math

import jax
import jax.numpy as jnp
from jax import lax
from jax.experimental import pallas as pl
from jax.experimental.pallas import tpu as pltpu

F32 = jnp.float32
BF16 = jnp.bfloat16

HEAD_DIM = 128
MOBA_HEADS = 8
MOBA_BLOCK = 256
MOBA_TOPK = 3
DN_HEADS = 8
DN_CONV = 4
DN_CHUNK = 64
NORM_EPS = 1e-6

LANES = 128
SUBLANES = 8
VMEM_LIMIT = 48 * 1024 * 1024

HI = lax.Precision.HIGHEST
NT_DIMS = (((1,), (1,)), ((), ()))


def _sigmoid(x):
    return 1.0 / (1.0 + jnp.exp(-x))


def _inproj_kernel(x_ref, gain_ref, w_ref, ws_ref, out_ref, small_ref, h_scr, *,
                   silu_tiles, sigmoid_from):
    j = pl.program_id(1)

    @pl.when(j == 0)
    def _():
        x = x_ref[...]
        inv = lax.rsqrt(jnp.mean(x * x, axis=-1, keepdims=True) + NORM_EPS)
        h = (x * inv * gain_ref[...]).astype(BF16)
        h_scr[...] = h
        small_ref[...] = jnp.dot(h, ws_ref[...], preferred_element_type=F32)

    acc = jnp.dot(h_scr[...], w_ref[...], preferred_element_type=F32)
    is_silu = functools.reduce(jnp.logical_or, [j == t for t in silu_tiles])
    is_sig = j >= sigmoid_from

    @pl.when(is_silu)
    def _():
        out_ref[...] = (acc * _sigmoid(acc)).astype(out_ref.dtype)

    @pl.when(is_sig)
    def _():
        out_ref[...] = _sigmoid(acc).astype(out_ref.dtype)

    @pl.when(jnp.logical_not(jnp.logical_or(is_silu, is_sig)))
    def _():
        out_ref[...] = acc.astype(out_ref.dtype)


def _inproj(x2, gain, w_main, w_small, *, tm, tn, silu_tiles, sigmoid_from):
    m, d = x2.shape
    n = w_main.shape[1]
    kern = functools.partial(_inproj_kernel, silu_tiles=silu_tiles, sigmoid_from=sigmoid_from)
    return pl.pallas_call(
        kern,
        grid=(m // tm, n // tn),
        in_specs=[
            pl.BlockSpec((tm, d), lambda i, j: (i, 0)),
            pl.BlockSpec((1, d), lambda i, j: (0, 0)),
            pl.BlockSpec((d, tn), lambda i, j: (0, j)),
            pl.BlockSpec((d, LANES), lambda i, j: (0, 0)),
        ],
        out_specs=[
            pl.BlockSpec((tm, tn), lambda i, j: (i, j)),
            pl.BlockSpec((tm, LANES), lambda i, j: (i, 0)),
        ],
        out_shape=[
            jax.ShapeDtypeStruct((m, n), BF16),
            jax.ShapeDtypeStruct((m, LANES), F32),
        ],
        scratch_shapes=[pltpu.VMEM((tm, d), BF16)],
        compiler_params=pltpu.CompilerParams(
            dimension_semantics=("arbitrary", "arbitrary"),
            vmem_limit_bytes=VMEM_LIMIT),
        name="inproj",
    )(x2, gain, w_main, w_small)


def _moba_kernel(q_ref, k_ref, v_ref, z_ref, o_ref, kmean_scr, vt_scr, sel_scr, *,
                 blk, n_blk, n_top, scale):
    h = pl.program_id(1)
    qi = pl.program_id(2)
    neg_inf = jnp.float32(-jnp.inf)

    @pl.when(qi == 0)
    def _():
        for n in range(n_blk):
            kf = k_ref[n * blk:(n + 1) * blk, :].astype(F32)
            kmean_scr[n:n + 1, :] = jnp.mean(kf, axis=0, keepdims=True)
            vf = v_ref[n * blk:(n + 1) * blk, :].astype(F32)
            vt_scr[n] = vf.T.astype(BF16)

    qf = q_ref[...].astype(F32) * scale
    qs = qf.astype(BF16)

    gate = lax.dot_general(kmean_scr[...], qf, NT_DIMS, precision=HI,
                           preferred_element_type=F32)
    row = lax.broadcasted_iota(jnp.int32, (n_blk, blk), 0)
    past = row < qi
    gm = jnp.where(past, gate, neg_inf)
    rank = jnp.zeros((n_blk, blk), jnp.int32)
    for m in range(n_blk):
        gm_m = gm[m:m + 1, :]
        beats = jnp.logical_or(gm_m > gm, jnp.logical_and(gm_m == gm, m < row))
        rank = rank + beats.astype(jnp.int32)
    sel = jnp.logical_and(past, rank < n_top)
    sel_scr[...] = jnp.where(sel, 0.0, neg_inf)

    slope = jnp.exp2(-(h + 1).astype(F32) * jnp.ones((1, 1), F32))
    kk = lax.broadcasted_iota(jnp.int32, (blk, blk), 0)
    qq = lax.broadcasted_iota(jnp.int32, (blk, blk), 1)
    rel = (qq - kk).astype(F32)

    k_own = k_ref[pl.ds(pl.multiple_of(qi * blk, blk), blk), :]
    s = lax.dot_general(k_own, qs, NT_DIMS, preferred_element_type=F32)
    s = jnp.where(kk <= qq, s - slope * rel, neg_inf)
    m0 = jnp.max(s, axis=0, keepdims=True)
    p = jnp.exp(s - m0)
    l0 = jnp.sum(p, axis=0, keepdims=True)
    acc0 = jnp.dot(vt_scr[qi], p.astype(BF16), preferred_element_type=F32)

    def body(n, carry):
        m_i, l_i, acc = carry
        k_n = k_ref[pl.ds(pl.multiple_of(n * blk, blk), blk), :]
        s = lax.dot_general(k_n, qs, NT_DIMS, preferred_element_type=F32)
        dist = rel + ((qi - n) * blk).astype(F32)
        s = s - slope * dist + sel_scr[pl.ds(n, 1), :]
        m_new = jnp.maximum(m_i, jnp.max(s, axis=0, keepdims=True))
        alpha = jnp.exp(m_i - m_new)
        p = jnp.exp(s - m_new)
        l_new = alpha * l_i + jnp.sum(p, axis=0, keepdims=True)
        acc_new = alpha * acc + jnp.dot(vt_scr[n], p.astype(BF16), preferred_element_type=F32)
        return m_new, l_new, acc_new

    _, l_f, acc_f = lax.fori_loop(0, qi, body, (m0, l0, acc0))
    out = (acc_f / l_f).T
    o_ref[...] = (out * z_ref[...].astype(F32)).astype(o_ref.dtype)


def _moba(proj, *, batch, seq, q_col, k_col, v_col, z_col):
    blk = MOBA_BLOCK
    assert seq % blk == 0
    n_blk = seq // blk
    n_top = min(MOBA_TOPK, n_blk - 1)
    dh = HEAD_DIM
    kern = functools.partial(_moba_kernel, blk=blk, n_blk=n_blk, n_top=n_top, scale=dh ** -0.5)
    return pl.pallas_call(
        kern,
        grid=(batch, MOBA_HEADS, n_blk),
        in_specs=[
            pl.BlockSpec((blk, dh), lambda b, h, i: (b * n_blk + i, q_col + h)),
            pl.BlockSpec((seq, dh), lambda b, h, i: (b, k_col + h)),
            pl.BlockSpec((seq, dh), lambda b, h, i: (b, v_col + h)),
            pl.BlockSpec((blk, dh), lambda b, h, i: (b * n_blk + i, z_col + h)),
        ],
        out_specs=pl.BlockSpec((blk, dh), lambda b, h, i: (b * n_blk + i, h)),
        out_shape=jax.ShapeDtypeStruct((batch * seq, MOBA_HEADS * dh), BF16),
        scratch_shapes=[
            pltpu.VMEM((n_blk, dh), F32),
            pltpu.VMEM((n_blk, dh, blk), BF16),
            pltpu.VMEM((n_blk, blk), F32),
        ],
        compiler_params=pltpu.CompilerParams(
            dimension_semantics=("arbitrary", "arbitrary", "arbitrary"),
            vmem_limit_bytes=VMEM_LIMIT),
        name="moba",
    )(proj, proj, proj, proj)


def _deltanet_kernel(q_ref, k_ref, v_ref, z_ref, sm_ref, cw_ref, alog_ref, dtb_ref, dng_ref,
                     o_ref, xs, st, *, n_heads, chunk):
    c = pl.program_id(1)
    C = chunk
    dk = HEAD_DIM
    W = n_heads * dk
    pad = SUBLANES
    neg_inf = jnp.float32(-jnp.inf)

    @pl.when(c == 0)
    def _():
        xs[0:pad, :] = jnp.zeros((pad, 3 * W), F32)
        st[...] = jnp.zeros_like(st)

    xs[pad:pad + C, 0:W] = q_ref[...].astype(F32)
    xs[pad:pad + C, W:2 * W] = k_ref[...].astype(F32)
    xs[pad:pad + C, 2 * W:3 * W] = v_ref[...].astype(F32)
    cw = cw_ref[...]
    y = None
    for i in range(DN_CONV):
        off = pad - (DN_CONV - 1) + i
        term = cw[i:i + 1, :] * xs[off:off + C, :]
        y = term if y is None else y + term
    y = y * _sigmoid(y)
    xs[0:pad, :] = xs[C:C + pad, :]

    sm = sm_ref[...]
    beta = _sigmoid(sm)
    a = sm + dtb_ref[...]
    softplus = jnp.maximum(a, 0.0) + jnp.log1p(jnp.exp(-jnp.abs(a)))
    g = -jnp.exp(alog_ref[...]) * softplus
    r_c = lax.broadcasted_iota(jnp.int32, (C, C), 0)
    c_c = lax.broadcasted_iota(jnp.int32, (C, C), 1)
    tri = (r_c >= c_c).astype(F32)
    gc = jnp.dot(tri, g, precision=HI, preferred_element_type=F32)
    gct = gc.T
    gl = gc[C - 1:C, :]
    eg = jnp.exp(gc)
    egl = jnp.exp(gl - gc)
    egl_last = jnp.exp(gl)

    R = lax.broadcasted_iota(jnp.int32, (2 * C, 2 * C), 0)
    Cc = lax.broadcasted_iota(jnp.int32, (2 * C, 2 * C), 1)
    same = (R >= C) == (Cc >= C)
    causal = jnp.logical_and(same, R >= Cc)
    strict = jnp.logical_and(same, R > Cc)
    eye = (R == Cc).astype(F32)
    dng = dng_ref[...]

    def col(arr, la, lb):
        return jnp.concatenate([arr[:, la:la + 1], arr[:, lb:lb + 1]], axis=0)

    def stack(base, ha, hb):
        return jnp.concatenate([y[:, base + ha * dk:base + (ha + 1) * dk],
                                y[:, base + hb * dk:base + (hb + 1) * dk]], axis=0)

    for pr in range(n_heads // 2):
        ha, hb = 2 * pr, 2 * pr + 1
        beta_c = col(beta, ha, hb)
        gc_c = col(gc, n_heads + ha, n_heads + hb)
        eg_c = col(eg, n_heads + ha, n_heads + hb)
        egl_c = col(egl, n_heads + ha, n_heads + hb)
        gc_r = jnp.concatenate([gct[n_heads + ha:n_heads + ha + 1, :],
                                gct[n_heads + hb:n_heads + hb + 1, :]], axis=1)
        q2 = stack(0, ha, hb)
        k2 = stack(W, ha, hb)
        v2 = stack(2 * W, ha, hb)
        qn = q2 * (lax.rsqrt(jnp.sum(q2 * q2, axis=-1, keepdims=True) + NORM_EPS) * dk ** -0.5)
        kn = k2 * lax.rsqrt(jnp.sum(k2 * k2, axis=-1, keepdims=True) + NORM_EPS)
        kb = kn * beta_c
        vb = v2 * beta_c
        dec = jnp.exp(jnp.where(causal, gc_c - gc_r, neg_inf))
        kn_b = kn.astype(BF16)
        kk = lax.dot_general(kb.astype(BF16), kn_b, NT_DIMS, preferred_element_type=F32)
        low = jnp.where(strict, kk * dec, 0.0)
        mp = -low
        tm = eye + mp
        for _ in range(int(math.log2(C)) - 1):
            mp = jnp.dot(mp, mp, precision=HI, preferred_element_type=F32)
            tm = tm + jnp.dot(tm, mp, precision=HI, preferred_element_type=F32)
        tm_b = tm.astype(BF16)
        u = jnp.dot(tm_b, vb.astype(BF16), preferred_element_type=F32)
        w = jnp.dot(tm_b, (kb * eg_c).astype(BF16), preferred_element_type=F32)
        qk = lax.dot_general(qn.astype(BF16), kn_b, NT_DIMS, preferred_element_type=F32)
        intra = jnp.where(causal, qk * dec, 0.0)

        s_a = st[ha]
        s_b = st[hb]
        s_ab = s_a.astype(BF16)
        s_bb = s_b.astype(BF16)
        w_b = w.astype(BF16)
        ws = jnp.concatenate([jnp.dot(w_b[:C], s_ab, preferred_element_type=F32),
                              jnp.dot(w_b[C:], s_bb, preferred_element_type=F32)], axis=0)
        v_new = u - ws
        qe = (qn * eg_c).astype(BF16)
        qs = jnp.concatenate([jnp.dot(qe[:C], s_ab, preferred_element_type=F32),
                              jnp.dot(qe[C:], s_bb, preferred_element_type=F32)], axis=0)
        vn_b = v_new.astype(BF16)
        out = qs + jnp.dot(intra.astype(BF16), vn_b, preferred_element_type=F32)
        kdt = (kn * egl_c).T.astype(BF16)
        la, lb = n_heads + ha, n_heads + hb
        st[ha] = s_a * egl_last[:, la:la + 1] + jnp.dot(kdt[:, :C], vn_b[:C],
                                                         preferred_element_type=F32)
        st[hb] = s_b * egl_last[:, lb:lb + 1] + jnp.dot(kdt[:, C:], vn_b[C:],
                                                         preferred_element_type=F32)

        o = out * lax.rsqrt(jnp.mean(out * out, axis=-1, keepdims=True) + NORM_EPS) * dng
        o_ref[:, ha * dk:(ha + 1) * dk] = (
            o[:C] * z_ref[:, ha * dk:(ha + 1) * dk].astype(F32)).astype(o_ref.dtype)
        o_ref[:, hb * dk:(hb + 1) * dk] = (
            o[C:] * z_ref[:, hb * dk:(hb + 1) * dk].astype(F32)).astype(o_ref.dtype)


def _deltanet(proj, small, conv_w, alog_row, dtb_row, dng_row, *, batch, seq,
              q_col, k_col, v_col, z_col):
    C = DN_CHUNK
    W = DN_HEADS * HEAD_DIM
    assert seq % C == 0
    n_chunk = seq // C
    kern = functools.partial(_deltanet_kernel, n_heads=DN_HEADS, chunk=C)
    row_map = lambda col: (lambda b, c: (b * n_chunk + c, col))
    const = lambda b, c: (0, 0)
    return pl.pallas_call(
        kern,
        grid=(batch, n_chunk),
        in_specs=[
            pl.BlockSpec((C, W), row_map(q_col)),
            pl.BlockSpec((C, W), row_map(k_col)),
            pl.BlockSpec((C, W), row_map(v_col)),
            pl.BlockSpec((C, W), row_map(z_col)),
            pl.BlockSpec((C, LANES), row_map(0)),
            pl.BlockSpec((DN_CONV, 3 * W), const),
            pl.BlockSpec((1, LANES), const),
            pl.BlockSpec((1, LANES), const),
            pl.BlockSpec((1, HEAD_DIM), const),
        ],
        out_specs=pl.BlockSpec((C, W), row_map(0)),
        out_shape=jax.ShapeDtypeStruct((batch * seq, W), BF16),
        scratch_shapes=[
            pltpu.VMEM((C + SUBLANES, 3 * W), F32),
            pltpu.VMEM((DN_HEADS, HEAD_DIM, HEAD_DIM), F32),
        ],
        compiler_params=pltpu.CompilerParams(
            dimension_semantics=("arbitrary", "arbitrary"),
            vmem_limit_bytes=VMEM_LIMIT),
        name="deltanet",
    )(proj, proj, proj, proj, small, conv_w, alog_row, dtb_row, dng_row)


def _merge_kernel(ya_ref, yb_ref, ga_ref, gb_ref, x_ref, wa_ref, wb_ref, wo_ref, fg_ref,
                  o_ref, *, final_norm):
    pa = jnp.dot(ya_ref[...], wa_ref[...], preferred_element_type=F32)
    pb = jnp.dot(yb_ref[...], wb_ref[...], preferred_element_type=F32)
    merged = ga_ref[...].astype(F32) * pa + gb_ref[...].astype(F32) * pb
    xn = x_ref[...] + jnp.dot(merged.astype(BF16), wo_ref[...], preferred_element_type=F32)
    if final_norm:
        xn = xn * lax.rsqrt(jnp.mean(xn * xn, axis=-1, keepdims=True) + NORM_EPS) * fg_ref[...]
    o_ref[...] = xn


def _merge(ya, yb, proj, x2, wa, wb, wo, fgain, *, tm, ga_col, gb_col, final_norm):
    m, d = x2.shape
    wa_n = ya.shape[1]
    wb_n = yb.shape[1]
    kern = functools.partial(_merge_kernel, final_norm=final_norm)
    const = lambda i: (0, 0)
    resident = functools.partial(pl.BlockSpec, index_map=const, pipeline_mode=pl.Buffered(1))
    return pl.pallas_call(
        kern,
        grid=(m // tm,),
        in_specs=[
            pl.BlockSpec((tm, wa_n), lambda i: (i, 0)),
            pl.BlockSpec((tm, wb_n), lambda i: (i, 0)),
            pl.BlockSpec((tm, d), lambda i: (i, ga_col)),
            pl.BlockSpec((tm, d), lambda i: (i, gb_col)),
            pl.BlockSpec((tm, d), lambda i: (i, 0)),
            resident((wa_n, d)),
            resident((wb_n, d)),
            resident((d, d)),
            pl.BlockSpec((1, d), const),
        ],
        out_specs=pl.BlockSpec((tm, d), lambda i: (i, 0)),
        out_shape=jax.ShapeDtypeStruct((m, d), F32),
        compiler_params=pltpu.CompilerParams(
            dimension_semantics=("arbitrary",),
            vmem_limit_bytes=VMEM_LIMIT),
        name="merge",
    )(ya, yb, proj, proj, x2, wa, wb, wo, fgain)


def _lane_row(vec, offset):
    return jnp.zeros((1, LANES), F32).at[0, offset:offset + vec.shape[0]].set(vec.astype(F32))


def kernel(x, norm_gain, w_in, conv_w, a_log, dt_bias, dn_norm_gain, w_branch_a, w_branch_b,
           w_out, final_norm_gain):
    batch, seq, d = x.shape
    depth = norm_gain.shape[0]
    mw = MOBA_HEADS * HEAD_DIM
    dw = DN_HEADS * HEAD_DIM
    assert mw == dw and d % mw == 0
    n_wide = 4 * mw + 4 * dw
    n_scalar = 2 * DN_HEADS
    tn = mw
    silu_tiles = (3, 7)
    sigmoid_from = n_wide // tn
    x2 = x.reshape(batch * seq, d)
    for layer in range(depth):
        w = w_in[layer]
        w_main = jnp.concatenate([w[:, :n_wide], w[:, n_wide + n_scalar:]], axis=1).astype(BF16)
        w_small = jnp.pad(w[:, n_wide:n_wide + n_scalar], ((0, 0), (0, LANES - n_scalar))).astype(BF16)
        proj, small = _inproj(x2, norm_gain[layer][None, :], w_main, w_small,
                              tm=512, tn=tn, silu_tiles=silu_tiles, sigmoid_from=sigmoid_from)
        ya = _moba(proj, batch=batch, seq=seq, q_col=0, k_col=MOBA_HEADS, v_col=2 * MOBA_HEADS,
                   z_col=3 * MOBA_HEADS)
        yb = _deltanet(proj, small, conv_w[layer],
                       _lane_row(a_log[layer], DN_HEADS), _lane_row(dt_bias[layer], DN_HEADS),
                       dn_norm_gain[layer][None, :].astype(F32),
                       batch=batch, seq=seq, q_col=4, k_col=5, v_col=6, z_col=7)
        x2 = _merge(ya, yb, proj, x2,
                    w_branch_a[layer].astype(BF16), w_branch_b[layer].astype(BF16),
                    w_out[layer].astype(BF16), final_norm_gain[None, :],
                    tm=256, ga_col=n_wide // d, gb_col=n_wide // d + 1,
                    final_norm=(layer == depth - 1))
    return x2.reshape(batch, seq, d)
```

```python
import functools
import math

import jax
import jax.numpy as jnp
from jax import lax
from jax.experimental import pallas as pl
from jax.experimental.pallas import tpu as pltpu

F32 = jnp.float32
BF16 = jnp.bfloat16

HEAD_DIM = 128
MOBA_HEADS = 8
MOBA_BLOCK = 256
MOBA_TOPK = 3
DN_HEADS = 8
DN_CONV = 4
DN_CHUNK = 64
NORM_EPS = 1e-6

LANES = 128
SUBLANES = 8
VMEM_LIMIT = 48 * 1024 * 1024

HI = lax.Precision.HIGHEST
NT_DIMS = (((1,), (1,)), ((), ()))


def _sigmoid(x):
    return 1.0 / (1.0 + jnp.exp(-x))


def _inproj_kernel(x_ref, gain_ref, w_ref, ws_ref, out_ref, small_ref, h_scr, *,
                   silu_tiles, sigmoid_from):
    j = pl.program_id(1)

    @pl.when(j == 0)
    def _():
        x = x_ref[...]
        inv = lax.rsqrt(jnp.mean(x * x, axis=-1, keepdims=True) + NORM_EPS)
        h = (x * inv * gain_ref[...]).astype(BF16)
        h_scr[...] = h
        small_ref[...] = jnp.dot(h, ws_ref[...], preferred_element_type=F32)

    acc = jnp.dot(h_scr[...], w_ref[...], preferred_element_type=F32)
    is_silu = functools.reduce(jnp.logical_or, [j == t for t in silu_tiles])
    is_sig = j >= sigmoid_from

    @pl.when(is_silu)
    def _():
        out_ref[...] = (acc * _sigmoid(acc)).astype(out_ref.dtype)

    @pl.when(is_sig)
    def _():
        out_ref[...] = _sigmoid(acc).astype(out_ref.dtype)

    @pl.when(jnp.logical_not(jnp.logical_or(is_silu, is_sig)))
    def _():
        out_ref[...] = acc.astype(out_ref.dtype)


def _inproj(x2, gain, w_main, w_small, *, tm, tn, silu_tiles, sigmoid_from):
    m, d = x2.shape
    n = w_main.shape[1]
    kern = functools.partial(_inproj_kernel, silu_tiles=silu_tiles, sigmoid_from=sigmoid_from)
    return pl.pallas_call(
        kern,
        grid=(m // tm, n // tn),
        in_specs=[
            pl.BlockSpec((tm, d), lambda i, j: (i, 0)),
            pl.BlockSpec((1, d), lambda i, j: (0, 0)),
            pl.BlockSpec((d, tn), lambda i, j: (0, j)),
            pl.BlockSpec((d, LANES), lambda i, j: (0, 0)),
        ],
        out_specs=[
            pl.BlockSpec((tm, tn), lambda i, j: (i, j)),
            pl.BlockSpec((tm, LANES), lambda i, j: (i, 0)),
        ],
        out_shape=[
            jax.ShapeDtypeStruct((m, n), BF16),
            jax.ShapeDtypeStruct((m, LANES), F32),
        ],
        scratch_shapes=[pltpu.VMEM((tm, d), BF16)],
        compiler_params=pltpu.CompilerParams(
            dimension_semantics=("arbitrary", "arbitrary"),
            vmem_limit_bytes=VMEM_LIMIT),
        name="inproj",
    )(x2, gain, w_main, w_small)


def _moba_kernel(q_ref, k_ref, v_ref, z_ref, o_ref, kmean_scr, vt_scr, sel_scr, *,
                 blk, n_blk, n_top, scale):
    h = pl.program_id(1)
    qi = pl.program_id(2)
    neg_inf = jnp.float32(-jnp.inf)

    @pl.when(qi == 0)
    def _():
        for n in range(n_blk):
            kf = k_ref[n * blk:(n + 1) * blk, :].astype(F32)
            kmean_scr[n:n + 1, :] = jnp.mean(kf, axis=0, keepdims=True)
            vf = v_ref[n * blk:(n + 1) * blk, :].astype(F32)
            vt_scr[n] = vf.T.astype(BF16)

    qf = q_ref[...].astype(F32) * scale
    qs = qf.astype(BF16)

    gate = lax.dot_general(kmean_scr[...], qf, NT_DIMS, precision=HI,
                           preferred_element_type=F32)
    row = lax.broadcasted_iota(jnp.int32, (n_blk, blk), 0)
    past = row < qi
    gm = jnp.where(past, gate, neg_inf)
    rank = jnp.zeros((n_blk, blk), jnp.int32)
    for m in range(n_blk):
        gm_m = gm[m:m + 1, :]
        beats = jnp.logical_or(gm_m > gm, jnp.logical_and(gm_m == gm, m < row))
        rank = rank + beats.astype(jnp.int32)
    sel = jnp.logical_and(past, rank < n_top)
    sel_scr[...] = jnp.where(sel, 0.0, neg_inf)

    slope = jnp.exp2(-(h + 1).astype(F32) * jnp.ones((1, 1), F32))
    kk = lax.broadcasted_iota(jnp.int32, (blk, blk), 0)
    qq = lax.broadcasted_iota(jnp.int32, (blk, blk), 1)
    rel = (qq - kk).astype(F32)

    k_own = k_ref[pl.ds(pl.multiple_of(qi * blk, blk), blk), :]
    s = lax.dot_general(k_own, qs, NT_DIMS, preferred_element_type=F32)
    s = jnp.where(kk <= qq, s - slope * rel, neg_inf)
    m0 = jnp.max(s, axis=0, keepdims=True)
    p = jnp.exp(s - m0)
    l0 = jnp.sum(p, axis=0, keepdims=True)
    acc0 = jnp.dot(vt_scr[qi], p.astype(BF16), preferred_element_type=F32)

    def body(n, carry):
        m_i, l_i, acc = carry
        k_n = k_ref[pl.ds(pl.multiple_of(n * blk, blk), blk), :]
        s = lax.dot_general(k_n, qs, NT_DIMS, preferred_element_type=F32)
        dist = rel + ((qi - n) * blk).astype(F32)
        s = s - slope * dist + sel_scr[pl.ds(n, 1), :]
        m_new = jnp.maximum(m_i, jnp.max(s, axis=0, keepdims=True))
        alpha = jnp.exp(m_i - m_new)
        p = jnp.exp(s - m_new)
        l_new = alpha * l_i + jnp.sum(p, axis=0, keepdims=True)
        acc_new = alpha * acc + jnp.dot(vt_scr[n], p.astype(BF16), preferred_element_type=F32)
        return m_new, l_new, acc_new

    _, l_f, acc_f = lax.fori_loop(0, qi, body, (m0, l0, acc0))
    out = (acc_f / l_f).T
    o_ref[...] = (out * z_ref[...].astype(F32)).astype(o_ref.dtype)


def _moba(proj, *, batch, seq, q_col, k_col, v_col, z_col):
    blk = MOBA_BLOCK
    assert seq % blk == 0
    n_blk = seq // blk
    n_top = min(MOBA_TOPK, n_blk - 1)
    dh = HEAD_DIM
    kern = functools.partial(_moba_kernel, blk=blk, n_blk=n_blk, n_top=n_top, scale=dh ** -0.5)
    return pl.pallas_call(
        kern,
        grid=(batch, MOBA_HEADS, n_blk),
        in_specs=[
            pl.BlockSpec((blk, dh), lambda b, h, i: (b * n_blk + i, q_col + h)),
            pl.BlockSpec((seq, dh), lambda b, h, i: (b, k_col + h)),
            pl.BlockSpec((seq, dh), lambda b, h, i: (b, v_col + h)),
            pl.BlockSpec((blk, dh), lambda b, h, i: (b * n_blk + i, z_col + h)),
        ],
        out_specs=pl.BlockSpec((blk, dh), lambda b, h, i: (b * n_blk + i, h)),
        out_shape=jax.ShapeDtypeStruct((batch * seq, MOBA_HEADS * dh), BF16),
        scratch_shapes=[
            pltpu.VMEM((n_blk, dh), F32),
            pltpu.VMEM((n_blk, dh, blk), BF16),
            pltpu.VMEM((n_blk, blk), F32),
        ],
        compiler_params=pltpu.CompilerParams(
            dimension_semantics=("arbitrary", "arbitrary", "arbitrary"),
            vmem_limit_bytes=VMEM_LIMIT),
        name="moba",
    )(proj, proj, proj, proj)


def _split(x):
    hi = x.astype(BF16)
    return hi, (x - hi.astype(F32)).astype(BF16)


def _dot3(a_hi, a_lo, b_hi, b_lo):
    n = b_hi.shape[1]
    r = jnp.dot(a_hi, jnp.concatenate([b_hi, b_lo], axis=1), preferred_element_type=F32)
    return r[:, :n] + r[:, n:] + jnp.dot(a_lo, b_hi, preferred_element_type=F32)


def _deltanet_kernel(q_ref, k_ref, v_ref, z_ref, sm_ref, cw_ref, alog_ref, dtb_ref, dng_ref,
                     o_ref, xs, st, *, n_heads, chunk):
    c = pl.program_id(1)
    C = chunk
    dk = HEAD_DIM
    W = n_heads * dk
    pad = SUBLANES
    neg_inf = jnp.float32(-jnp.inf)

    @pl.when(c == 0)
    def _():
        xs[0:pad, :] = jnp.zeros((pad, 3 * W), F32)
        st[...] = jnp.zeros_like(st)

    xs[pad:pad + C, 0:W] = q_ref[...].astype(F32)
    xs[pad:pad + C, W:2 * W] = k_ref[...].astype(F32)
    xs[pad:pad + C, 2 * W:3 * W] = v_ref[...].astype(F32)
    cw = cw_ref[...]
    y = None
    for i in range(DN_CONV):
        off = pad - (DN_CONV - 1) + i
        term = cw[i:i + 1, :] * xs[off:off + C, :]
        y = term if y is None else y + term
    y = y * _sigmoid(y)
    xs[0:pad, :] = xs[C:C + pad, :]

    sm = sm_ref[...]
    beta = _sigmoid(sm)
    a = sm + dtb_ref[...]
    softplus = jnp.maximum(a, 0.0) + jnp.log1p(jnp.exp(-jnp.abs(a)))
    g = -jnp.exp(alog_ref[...]) * softplus
    r_c = lax.broadcasted_iota(jnp.int32, (C, C), 0)
    c_c = lax.broadcasted_iota(jnp.int32, (C, C), 1)
    tri = (r_c >= c_c).astype(F32)
    gc = jnp.dot(tri, g, precision=HI, preferred_element_type=F32)
    gct = gc.T
    gl = gc[C - 1:C, :]
    eg = jnp.exp(gc)
    egl = jnp.exp(gl - gc)
    egl_last = jnp.exp(gl)

    R = lax.broadcasted_iota(jnp.int32, (2 * C, 2 * C), 0)
    Cc = lax.broadcasted_iota(jnp.int32, (2 * C, 2 * C), 1)
    same = (R >= C) == (Cc >= C)
    causal = jnp.logical_and(same, R >= Cc)
    strict = jnp.logical_and(same, R > Cc)
    lvl = jnp.where(strict, 31 - lax.clz(R ^ Cc), -1)
    eye = (R == Cc).astype(F32)
    dng = dng_ref[...]
    n_lvl = int(math.log2(C))

    def col(arr, la, lb):
        return jnp.concatenate([arr[:, la:la + 1], arr[:, lb:lb + 1]], axis=0)

    def stack(base, ha, hb):
        return jnp.concatenate([y[:, base + ha * dk:base + (ha + 1) * dk],
                                y[:, base + hb * dk:base + (hb + 1) * dk]], axis=0)

    pairs = [(2 * p, 2 * p + 1) for p in range(n_heads // 2)]
    pre = []
    for ha, hb in pairs:
        beta_c = col(beta, ha, hb)
        gc_c = col(gc, n_heads + ha, n_heads + hb)
        eg_c = col(eg, n_heads + ha, n_heads + hb)
        egl_c = col(egl, n_heads + ha, n_heads + hb)
        gc_r = jnp.concatenate([gct[n_heads + ha:n_heads + ha + 1, :],
                                gct[n_heads + hb:n_heads + hb + 1, :]], axis=1)
        q2 = stack(0, ha, hb)
        k2 = stack(W, ha, hb)
        v2 = stack(2 * W, ha, hb)
        qn = q2 * (lax.rsqrt(jnp.sum(q2 * q2, axis=-1, keepdims=True) + NORM_EPS) * dk ** -0.5)
        kn = k2 * lax.rsqrt(jnp.sum(k2 * k2, axis=-1, keepdims=True) + NORM_EPS)
        kb = kn * beta_c
        dec = jnp.exp(jnp.where(causal, gc_c - gc_r, neg_inf))
        kq = lax.dot_general(jnp.concatenate([kb, qn], axis=0).astype(BF16), kn.astype(BF16),
                             NT_DIMS, preferred_element_type=F32)
        low = jnp.where(strict, kq[:2 * C] * dec, 0.0)
        intra = jnp.where(causal, kq[2 * C:] * dec, 0.0).astype(BF16)
        rhs = jnp.concatenate([v2 * beta_c, kb * eg_c], axis=1).astype(BF16)
        qe = (qn * eg_c).astype(BF16)
        kdt = (kn * egl_c).T.astype(BF16)
        pre.append((low, intra, rhs, qe, kdt))

    lows = [_split(p[0]) for p in pre]
    tms = [eye - jnp.where(lvl == 0, p[0], 0.0) for p in pre]
    for b in range(1, n_lvl):
        sel = lvl == b
        ts = [_split(t) for t in tms]
        yys = [_dot3(jnp.where(sel, lh, jnp.zeros_like(lh)), jnp.where(sel, ll, jnp.zeros_like(ll)),
                     th, tl) for (lh, ll), (th, tl) in zip(lows, ts)]
        xs_ = [_dot3(th, tl, *_split(yy)) for (th, tl), yy in zip(ts, yys)]
        tms = [t - x for t, x in zip(tms, xs_)]

    uws = [jnp.dot(t.astype(BF16), p[2], preferred_element_type=F32) for t, p in zip(tms, pre)]
    for (ha, hb), p, uw in zip(pairs, pre, uws):
        _, intra, _, qe, kdt = p
        u = uw[:, :dk]
        w_b = uw[:, dk:].astype(BF16)
        s_a = st[ha]
        s_b = st[hb]
        wq_a = jnp.dot(jnp.concatenate([w_b[:C], qe[:C]], axis=0), s_a.astype(BF16),
                       preferred_element_type=F32)
        wq_b = jnp.dot(jnp.concatenate([w_b[C:], qe[C:]], axis=0), s_b.astype(BF16),
                       preferred_element_type=F32)
        v_new = u - jnp.concatenate([wq_a[:C], wq_b[:C]], axis=0)
        vn_b = v_new.astype(BF16)
        out = (jnp.concatenate([wq_a[C:], wq_b[C:]], axis=0)
               + jnp.dot(intra, vn_b, preferred_element_type=F32))
        la, lb = n_heads + ha, n_heads + hb
        st[ha] = s_a * egl_last[:, la:la + 1] + jnp.dot(kdt[:, :C], vn_b[:C],
                                                         preferred_element_type=F32)
        st[hb] = s_b * egl_last[:, lb:lb + 1] + jnp.dot(kdt[:, C:], vn_b[C:],
                                                         preferred_element_type=F32)

        o = out * lax.rsqrt(jnp.mean(out * out, axis=-1, keepdims=True) + NORM_EPS) * dng
        o_ref[:, ha * dk:(ha + 1) * dk] = (
            o[:C] * z_ref[:, ha * dk:(ha + 1) * dk].astype(F32)).astype(o_ref.dtype)
        o_ref[:, hb * dk:(hb + 1) * dk] = (
            o[C:] * z_ref[:, hb * dk:(hb + 1) * dk].astype(F32)).astype(o_ref.dtype)


def _deltanet(proj, small, conv_w, alog_row, dtb_row, dng_row, *, batch, seq,
              q_col, k_col, v_col, z_col):
    C = DN_CHUNK
    W = DN_HEADS * HEAD_DIM
    assert seq % C == 0
    n_chunk = seq // C
    kern = functools.partial(_deltanet_kernel, n_heads=DN_HEADS, chunk=C)
    row_map = lambda col: (lambda b, c: (b * n_chunk + c, col))
    const = lambda b, c: (0, 0)
    return pl.pallas_call(
        kern,
        grid=(batch, n_chunk),
        in_specs=[
            pl.BlockSpec((C, W), row_map(q_col)),
            pl.BlockSpec((C, W), row_map(k_col)),
            pl.BlockSpec((C, W), row_map(v_col)),
            pl.BlockSpec((C, W), row_map(z_col)),
            pl.BlockSpec((C, LANES), row_map(0)),
            pl.BlockSpec((DN_CONV, 3 * W), const),
            pl.BlockSpec((1, LANES), const),
            pl.BlockSpec((1, LANES), const),
            pl.BlockSpec((1, HEAD_DIM), const),
        ],
        out_specs=pl.BlockSpec((C, W), row_map(0)),
        out_shape=jax.ShapeDtypeStruct((batch * seq, W), BF16),
        scratch_shapes=[
            pltpu.VMEM((C + SUBLANES, 3 * W), F32),
            pltpu.VMEM((DN_HEADS, HEAD_DIM, HEAD_DIM), F32),
        ],
        compiler_params=pltpu.CompilerParams(
            dimension_semantics=("arbitrary", "arbitrary"),
            vmem_limit_bytes=VMEM_LIMIT),
        name="deltanet",
    )(proj, proj, proj, proj, small, conv_w, alog_row, dtb_row, dng_row)


def _merge_kernel(ya_ref, yb_ref, ga_ref, gb_ref, x_ref, wa_ref, wb_ref, wo_ref, fg_ref,
                  o_ref, *, final_norm):
    pa = jnp.dot(ya_ref[...], wa_ref[...], preferred_element_type=F32)
    pb = jnp.dot(yb_ref[...], wb_ref[...], preferred_element_type=F32)
    merged = ga_ref[...].astype(F32) * pa + gb_ref[...].astype(F32) * pb
    xn = x_ref[...] + jnp.dot(merged.astype(BF16), wo_ref[...], preferred_element_type=F32)
    if final_norm:
        xn = xn * lax.rsqrt(jnp.mean(xn * xn, axis=-1, keepdims=True) + NORM_EPS) * fg_ref[...]
    o_ref[...] = xn


def _merge(ya, yb, proj, x2, wa, wb, wo, fgain, *, tm, ga_col, gb_col, final_norm):
    m, d = x2.shape
    wa_n = ya.shape[1]
    wb_n = yb.shape[1]
    kern = functools.partial(_merge_kernel, final_norm=final_norm)
    const = lambda i: (0, 0)
    resident = functools.partial(pl.BlockSpec, index_map=const, pipeline_mode=pl.Buffered(1))
    return pl.pallas_call(
        kern,
        grid=(m // tm,),
        in_specs=[
            pl.BlockSpec((tm, wa_n), lambda i: (i, 0)),
            pl.BlockSpec((tm, wb_n), lambda i: (i, 0)),
            pl.BlockSpec((tm, d), lambda i: (i, ga_col)),
            pl.BlockSpec((tm, d), lambda i: (i, gb_col)),
            pl.BlockSpec((tm, d), lambda i: (i, 0)),
            resident((wa_n, d)),
            resident((wb_n, d)),
            resident((d, d)),
            pl.BlockSpec((1, d), const),
        ],
        out_specs=pl.BlockSpec((tm, d), lambda i: (i, 0)),
        out_shape=jax.ShapeDtypeStruct((m, d), F32),
        compiler_params=pltpu.CompilerParams(
            dimension_semantics=("arbitrary",),
            vmem_limit_bytes=VMEM_LIMIT),
        name="merge",
    )(ya, yb, proj, proj, x2, wa, wb, wo, fgain)


def _lane_row(vec, offset):
    return jnp.zeros((1, LANES), F32).at[0, offset:offset + vec.shape[0]].set(vec.astype(F32))


def kernel(x, norm_gain, w_in, conv_w, a_log, dt_bias, dn_norm_gain, w_branch_a, w_branch_b,
           w_out, final_norm_gain):
    batch, seq, d = x.shape
    depth = norm_gain.shape[0]
    mw = MOBA_HEADS * HEAD_DIM
    dw = DN_HEADS * HEAD_DIM
    assert mw == dw and d % mw == 0
    n_wide = 4 * mw + 4 * dw
    n_scalar = 2 * DN_HEADS
    tn = mw
    silu_tiles = (3, 7)
    sigmoid_from = n_wide // tn
    x2 = x.reshape(batch * seq, d)
    for layer in range(depth):
        w = w_in[layer]
        w_main = jnp.concatenate([w[:, :n_wide], w[:, n_wide + n_scalar:]], axis=1).astype(BF16)
        w_small = jnp.pad(w[:, n_wide:n_wide + n_scalar], ((0, 0), (0, LANES - n_scalar))).astype(BF16)
        proj, small = _inproj(x2, norm_gain[layer][None, :], w_main, w_small,
                              tm=512, tn=tn, silu_tiles=silu_tiles, sigmoid_from=sigmoid_from)
        ya = _moba(proj, batch=batch, seq=seq, q_col=0, k_col=MOBA_HEADS, v_col=2 * MOBA_HEADS,
                   z_col=3 * MOBA_HEADS)
        yb = _deltanet(proj, small, conv_w[layer],
                       _lane_row(a_log[layer], DN_HEADS), _lane_row(dt_bias[layer], DN_HEADS),
                       dn_norm_gain[layer][None, :].astype(F32),
                       batch=batch, seq=seq, q_col=4, k_col=5, v_col=6, z_col=7)
        x2 = _merge(ya, yb, proj, x2,
                    w_branch_a[layer].astype(BF16), w_branch_b[layer].astype(BF16),
                    w_out[layer].astype(BF16), final_norm_gain[None, :],
                    tm=256, ga_col=n_wide // d, gb_col=n_wide // d + 1,
                    final_norm=(layer == depth - 1))
    return x2.reshape(batch, seq, d)
```

```python
import functools
import math

import jax
import jax.numpy as jnp
from jax import lax
from jax.experimental import pallas as pl
from jax.experimental.pallas import tpu as pltpu

F32 = jnp.float32
BF16 = jnp.bfloat16

HEAD_DIM = 128
MOBA_HEADS = 8
MOBA_BLOCK = 256
MOBA_TOPK = 3
DN_HEADS = 8
DN_CONV = 4
DN_CHUNK = 64
NORM_EPS = 1e-6

LANES = 128
SUBLANES = 8
VMEM_LIMIT = 48 * 1024 * 1024

HI = lax.Precision.HIGHEST
NT_DIMS = (((1,), (1,)), ((), ()))


def _sigmoid(x):
    return 0.5 * (jnp.tanh(0.5 * x) + 1.0)


INPROJ_ROW_CHUNK = 256


def _inproj_kernel(x_ref, gain_ref, wa_ref, wg_ref, ws_ref, out_ref, small_ref, h_scr, *,
                   n_act_tiles, silu_tiles):
    j = pl.program_id(1)
    tm = h_scr.shape[0]
    chunks = [slice(r, r + INPROJ_ROW_CHUNK) for r in range(0, tm, INPROJ_ROW_CHUNK)]

    @pl.when(j == 0)
    def _():
        x = x_ref[...]
        inv = lax.rsqrt(jnp.mean(x * x, axis=-1, keepdims=True) + NORM_EPS)
        h = (x * inv * gain_ref[...]).astype(BF16)
        h_scr[...] = h
        small_ref[...] = jnp.dot(h, ws_ref[...], preferred_element_type=F32)

    @pl.when(j < n_act_tiles)
    def _():
        is_silu = functools.reduce(jnp.logical_or, [j == t for t in silu_tiles])
        for rows in chunks:
            acc = jnp.dot(h_scr[rows, :], wa_ref[...], preferred_element_type=F32)
            out_ref[rows, :] = (acc * jnp.where(is_silu, _sigmoid(acc), 1.0)).astype(out_ref.dtype)

    @pl.when(j >= n_act_tiles)
    def _():
        for rows in chunks:
            acc = jnp.dot(h_scr[rows, :], wg_ref[...], preferred_element_type=F32)
            out_ref[rows, :] = _sigmoid(acc).astype(out_ref.dtype)


def _inproj(x2, gain, w_act, w_gate, w_small, *, tm, tn, silu_tiles):
    m, d = x2.shape
    n_act_tiles = w_act.shape[1] // tn
    n_gate_tiles = w_gate.shape[1] // tn
    kern = functools.partial(_inproj_kernel, n_act_tiles=n_act_tiles, silu_tiles=silu_tiles)
    return pl.pallas_call(
        kern,
        grid=(m // tm, n_act_tiles + n_gate_tiles),
        in_specs=[
            pl.BlockSpec((tm, d), lambda i, j: (i, 0)),
            pl.BlockSpec((1, d), lambda i, j: (0, 0)),
            pl.BlockSpec((d, tn), lambda i, j: (0, jnp.minimum(j, n_act_tiles - 1))),
            pl.BlockSpec((d, tn), lambda i, j: (0, jnp.maximum(j - n_act_tiles, 0))),
            pl.BlockSpec((d, LANES), lambda i, j: (0, 0)),
        ],
        out_specs=[
            pl.BlockSpec((tm, tn), lambda i, j: (i, j)),
            pl.BlockSpec((tm, LANES), lambda i, j: (i, 0)),
        ],
        out_shape=[
            jax.ShapeDtypeStruct((m, (n_act_tiles + n_gate_tiles) * tn), BF16),
            jax.ShapeDtypeStruct((m, LANES), F32),
        ],
        scratch_shapes=[pltpu.VMEM((tm, d), BF16)],
        compiler_params=pltpu.CompilerParams(
            dimension_semantics=("arbitrary", "arbitrary"),
            vmem_limit_bytes=VMEM_LIMIT),
        name="inproj",
    )(x2, gain, w_act, w_gate, w_small)


N_SPLIT = 3
LOG2E = math.log2(math.e)


def _split3(x):
    parts = []
    for _ in range(N_SPLIT):
        p = x.astype(BF16)
        parts.append(p)
        x = x - p.astype(F32)
    return parts


def _moba_kernel(q_ref, k_ref, v_ref, z_ref, o_ref, kx_scr, qx_scr, vt_scr, sel_scr, s_scr, *,
                 blk, n_blk, n_top, scale):
    h = pl.program_id(1)
    seq, dh = q_ref.shape
    neg_inf = jnp.float32(-jnp.inf)
    shift = int(math.log2(blk))
    assert 1 << shift == blk
    ln = lax.broadcasted_iota(jnp.int32, (seq, dh), 1)

    @pl.when(jnp.logical_and(pl.program_id(0) == 0, h == 0))
    def _():
        r = lax.broadcasted_iota(jnp.int32, (seq, dh), 0)
        kx_scr[:, dh:] = jnp.where(ln < N_SPLIT, r & (blk - 1), 0).astype(F32).astype(BF16)

    kx_scr[:, :dh] = k_ref[...]
    kmeans = []
    for n in range(n_blk):
        kmeans.append(jnp.mean(k_ref[n * blk:(n + 1) * blk, :].astype(F32), axis=0, keepdims=True))
        vt_scr[n] = v_ref[n * blk:(n + 1) * blk, :].astype(F32).T.astype(BF16)
    kmean = jnp.concatenate(kmeans, axis=0)

    slope2 = jnp.exp2(-(h + 1).astype(F32) * jnp.ones((1, 1), F32)) * LOG2E
    c_parts = _split3(slope2)
    q_aug = jnp.zeros((seq, dh), F32)
    for i, c in enumerate(c_parts):
        q_aug = jnp.where(ln == i, c.astype(F32), q_aug)
    q_raw = q_ref[...]
    qx_scr[:, :dh] = (q_raw.astype(F32) * (scale * LOG2E)).astype(BF16)
    qx_scr[:, dh:] = q_aug.astype(BF16)

    km_parts = jnp.concatenate(_split3(kmean), axis=0)
    gparts = lax.dot_general(km_parts, q_raw, NT_DIMS, preferred_element_type=F32)
    gate = functools.reduce(
        lambda a, b: a + b,
        [gparts[i * n_blk:(i + 1) * n_blk] for i in reversed(range(N_SPLIT))])
    row = lax.broadcasted_iota(jnp.int32, (n_blk, seq), 0)
    qblk = lax.broadcasted_iota(jnp.int32, (n_blk, seq), 1) >> shift
    past = row < qblk
    gm = jnp.where(past, gate, neg_inf)
    rank = jnp.zeros((n_blk, seq), jnp.int32)
    for m in range(n_blk):
        gm_m = gm[m:m + 1, :]
        beats = jnp.logical_or(gm_m > gm, jnp.logical_and(gm_m == gm, m < row))
        rank = rank + beats.astype(jnp.int32)
    sel = jnp.logical_and(past, rank < n_top)
    sel_scr[...] = jnp.where(sel, -(slope2 * blk) * (qblk - row).astype(F32), neg_inf)

    kk = lax.broadcasted_iota(jnp.int32, (blk, blk), 0)
    qq = lax.broadcasted_iota(jnp.int32, (blk, blk), 1)
    causal = kk <= qq

    def pass_a(slot, qi, n, m_run):
        s = lax.dot_general(kx_scr[n * blk:(n + 1) * blk, :], qx_scr[qi * blk:(qi + 1) * blk, :],
                            NT_DIMS, preferred_element_type=F32)
        if n == qi:
            s = jnp.where(causal, s, neg_inf)
        else:
            s = s + sel_scr[n:n + 1, qi * blk:(qi + 1) * blk]
        s_scr[slot, n] = s
        m = jnp.max(s, axis=0, keepdims=True)
        return m if m_run is None else jnp.maximum(m_run, m)

    def pass_b(slot, n, m_fin, state):
        p = jnp.exp2(s_scr[slot, n] - m_fin)
        l_n = jnp.sum(p, axis=0, keepdims=True)
        pv = jnp.dot(vt_scr[n], p.astype(BF16), preferred_element_type=F32)
        return (l_n, pv) if state is None else (state[0] + l_n, state[1] + pv)

    def finish(qi, state):
        l_i, acc = state
        out = (acc * (1.0 / l_i)).T
        rows = slice(qi * blk, (qi + 1) * blk)
        o_ref[rows, :] = (out * z_ref[rows, :].astype(F32)).astype(o_ref.dtype)

    def key_blocks(qi):
        return [qi] + list(range(qi))

    pending = None
    for idx, qi in enumerate(reversed(range(n_blk))):
        slot = idx % 2
        a_steps = key_blocks(qi)
        b_steps = key_blocks(pending[1]) if pending is not None else []
        m_run, state = None, None
        for t in range(max(len(a_steps), len(b_steps))):
            if t < len(a_steps):
                m_run = pass_a(slot, qi, a_steps[t], m_run)
            if t < len(b_steps):
                state = pass_b(pending[0], b_steps[t], pending[2], state)
        if pending is not None:
            finish(pending[1], state)
        pending = (slot, qi, m_run)
    state = None
    for n in key_blocks(pending[1]):
        state = pass_b(pending[0], n, pending[2], state)
    finish(pending[1], state)


def _moba(proj, *, batch, seq, q_col, k_col, v_col, z_col):
    blk = MOBA_BLOCK
    assert seq % blk == 0
    n_blk = seq // blk
    n_top = min(MOBA_TOPK, n_blk - 1)
    dh = HEAD_DIM
    kern = functools.partial(_moba_kernel, blk=blk, n_blk=n_blk, n_top=n_top, scale=dh ** -0.5)
    col_map = lambda col: (lambda b, h: (b, col + h))
    return pl.pallas_call(
        kern,
        grid=(batch, MOBA_HEADS),
        in_specs=[
            pl.BlockSpec((seq, dh), col_map(q_col)),
            pl.BlockSpec((seq, dh), col_map(k_col)),
            pl.BlockSpec((seq, dh), col_map(v_col)),
            pl.BlockSpec((seq, dh), col_map(z_col)),
        ],
        out_specs=pl.BlockSpec((seq, dh), col_map(0)),
        out_shape=jax.ShapeDtypeStruct((batch * seq, MOBA_HEADS * dh), BF16),
        scratch_shapes=[
            pltpu.VMEM((seq, 2 * dh), BF16),
            pltpu.VMEM((seq, 2 * dh), BF16),
            pltpu.VMEM((n_blk, dh, blk), BF16),
            pltpu.VMEM((n_blk, seq), F32),
            pltpu.VMEM((2, n_blk, blk, blk), F32),
        ],
        compiler_params=pltpu.CompilerParams(
            dimension_semantics=("arbitrary", "arbitrary"),
            vmem_limit_bytes=VMEM_LIMIT),
        name="moba",
    )(proj, proj, proj, proj)


def _split(x):
    hi = x.astype(BF16)
    return hi, (x - hi.astype(F32)).astype(BF16)


def _dot3(a_hi, a_lo, b_hi, b_lo):
    n = b_hi.shape[1]
    r = jnp.dot(a_hi, jnp.concatenate([b_hi, b_lo], axis=1), preferred_element_type=F32)
    return r[:, :n] + r[:, n:] + jnp.dot(a_lo, b_hi, preferred_element_type=F32)


def _deltanet_kernel(q_ref, k_ref, v_ref, z_ref, sm_ref, cw_ref, alog_ref, dtb_ref, dng_ref,
                     o_ref, xs, st, *, n_heads, chunk):
    c = pl.program_id(1)
    C = chunk
    dk = HEAD_DIM
    W = n_heads * dk
    pad = SUBLANES
    neg_inf = jnp.float32(-jnp.inf)

    @pl.when(c == 0)
    def _():
        xs[0:pad, :] = jnp.zeros((pad, 3 * W), F32)
        st[...] = jnp.zeros_like(st)

    xs[pad:pad + C, 0:W] = q_ref[...].astype(F32)
    xs[pad:pad + C, W:2 * W] = k_ref[...].astype(F32)
    xs[pad:pad + C, 2 * W:3 * W] = v_ref[...].astype(F32)
    cw = cw_ref[...]
    y = None
    for i in range(DN_CONV):
        off = pad - (DN_CONV - 1) + i
        term = cw[i:i + 1, :] * xs[off:off + C, :]
        y = term if y is None else y + term
    y = y * _sigmoid(y)
    xs[0:pad, :] = xs[C:C + pad, :]

    sm = sm_ref[...]
    beta = _sigmoid(sm)
    a = sm + dtb_ref[...]
    softplus = jnp.maximum(a, 0.0) + jnp.log1p(jnp.exp(-jnp.abs(a)))
    g = -jnp.exp(alog_ref[...]) * softplus
    r_c = lax.broadcasted_iota(jnp.int32, (C, C), 0)
    c_c = lax.broadcasted_iota(jnp.int32, (C, C), 1)
    tri = (r_c >= c_c).astype(F32)
    gc = jnp.dot(tri, g, precision=HI, preferred_element_type=F32)
    gct = gc.T
    gl = gc[C - 1:C, :]
    eg = jnp.exp(gc)
    egl = jnp.exp(gl - gc)
    egl_last = jnp.exp(gl)

    R = lax.broadcasted_iota(jnp.int32, (2 * C, 2 * C), 0)
    Cc = lax.broadcasted_iota(jnp.int32, (2 * C, 2 * C), 1)
    same = (R >= C) == (Cc >= C)
    causal = jnp.logical_and(same, R >= Cc)
    strict = jnp.logical_and(same, R > Cc)
    lvl = jnp.where(strict, 31 - lax.clz(R ^ Cc), -1)
    eye = (R == Cc).astype(F32)
    dng = dng_ref[...]
    n_lvl = int(math.log2(C))

    def col(arr, la, lb):
        return jnp.concatenate([arr[:, la:la + 1], arr[:, lb:lb + 1]], axis=0)

    def stack(base, ha, hb):
        return jnp.concatenate([y[:, base + ha * dk:base + (ha + 1) * dk],
                                y[:, base + hb * dk:base + (hb + 1) * dk]], axis=0)

    pairs = [(2 * p, 2 * p + 1) for p in range(n_heads // 2)]
    pre = []
    for ha, hb in pairs:
        beta_c = col(beta, ha, hb)
        gc_c = col(gc, n_heads + ha, n_heads + hb)
        eg_c = col(eg, n_heads + ha, n_heads + hb)
        egl_c = col(egl, n_heads + ha, n_heads + hb)
        gc_r = jnp.concatenate([gct[n_heads + ha:n_heads + ha + 1, :],
                                gct[n_heads + hb:n_heads + hb + 1, :]], axis=1)
        q2 = stack(0, ha, hb)
        k2 = stack(W, ha, hb)
        v2 = stack(2 * W, ha, hb)
        qn = q2 * (lax.rsqrt(jnp.sum(q2 * q2, axis=-1, keepdims=True) + NORM_EPS) * dk ** -0.5)
        kn = k2 * lax.rsqrt(jnp.sum(k2 * k2, axis=-1, keepdims=True) + NORM_EPS)
        kb = kn * beta_c
        dec = jnp.exp(jnp.where(causal, gc_c - gc_r, neg_inf))
        kq = lax.dot_general(jnp.concatenate([kb, qn], axis=0).astype(BF16), kn.astype(BF16),
                             NT_DIMS, preferred_element_type=F32)
        low = jnp.where(strict, kq[:2 * C] * dec, 0.0)
        intra = jnp.where(causal, kq[2 * C:] * dec, 0.0).astype(BF16)
        rhs = jnp.concatenate([v2 * beta_c, kb * eg_c], axis=1).astype(BF16)
        qe = (qn * eg_c).astype(BF16)
        kdt = (kn * egl_c).T.astype(BF16)
        pre.append((low, intra, rhs, qe, kdt))

    lows = [_split(p[0]) for p in pre]
    tms = [eye - jnp.where(lvl == 0, p[0], 0.0) for p in pre]
    for b in range(1, n_lvl):
        sel = lvl == b
        ts = [_split(t) for t in tms]
        yys = [_dot3(jnp.where(sel, lh, jnp.zeros_like(lh)), jnp.where(sel, ll, jnp.zeros_like(ll)),
                     th, tl) for (lh, ll), (th, tl) in zip(lows, ts)]
        xs_ = [_dot3(th, tl, *_split(yy)) for (th, tl), yy in zip(ts, yys)]
        tms = [t - x for t, x in zip(tms, xs_)]

    uws = [jnp.dot(t.astype(BF16), p[2], preferred_element_type=F32) for t, p in zip(tms, pre)]
    for (ha, hb), p, uw in zip(pairs, pre, uws):
        _, intra, _, qe, kdt = p
        u = uw[:, :dk]
        w_b = uw[:, dk:].astype(BF16)
        s_a = st[ha]
        s_b = st[hb]
        wq_a = jnp.dot(jnp.concatenate([w_b[:C], qe[:C]], axis=0), s_a.astype(BF16),
                       preferred_element_type=F32)
        wq_b = jnp.dot(jnp.concatenate([w_b[C:], qe[C:]], axis=0), s_b.astype(BF16),
                       preferred_element_type=F32)
        v_new = u - jnp.concatenate([wq_a[:C], wq_b[:C]], axis=0)
        vn_b = v_new.astype(BF16)
        out = (jnp.concatenate([wq_a[C:], wq_b[C:]], axis=0)
               + jnp.dot(intra, vn_b, preferred_element_type=F32))
        la, lb = n_heads + ha, n_heads + hb
        st[ha] = s_a * egl_last[:, la:la + 1] + jnp.dot(kdt[:, :C], vn_b[:C],
                                                         preferred_element_type=F32)
        st[hb] = s_b * egl_last[:, lb:lb + 1] + jnp.dot(kdt[:, C:], vn_b[C:],
                                                         preferred_element_type=F32)

        o = out * lax.rsqrt(jnp.mean(out * out, axis=-1, keepdims=True) + NORM_EPS) * dng
        o_ref[:, ha * dk:(ha + 1) * dk] = (
            o[:C] * z_ref[:, ha * dk:(ha + 1) * dk].astype(F32)).astype(o_ref.dtype)
        o_ref[:, hb * dk:(hb + 1) * dk] = (
            o[C:] * z_ref[:, hb * dk:(hb + 1) * dk].astype(F32)).astype(o_ref.dtype)


def _deltanet(proj, small, conv_w, alog_row, dtb_row, dng_row, *, batch, seq,
              q_col, k_col, v_col, z_col):
    C = DN_CHUNK
    W = DN_HEADS * HEAD_DIM
    assert seq % C == 0
    n_chunk = seq // C
    kern = functools.partial(_deltanet_kernel, n_heads=DN_HEADS, chunk=C)
    row_map = lambda col: (lambda b, c: (b * n_chunk + c, col))
    const = lambda b, c: (0, 0)
    return pl.pallas_call(
        kern,
        grid=(batch, n_chunk),
        in_specs=[
            pl.BlockSpec((C, W), row_map(q_col)),
            pl.BlockSpec((C, W), row_map(k_col)),
            pl.BlockSpec((C, W), row_map(v_col)),
            pl.BlockSpec((C, W), row_map(z_col)),
            pl.BlockSpec((C, LANES), row_map(0)),
            pl.BlockSpec((DN_CONV, 3 * W), const),
            pl.BlockSpec((1, LANES), const),
            pl.BlockSpec((1, LANES), const),
            pl.BlockSpec((1, HEAD_DIM), const),
        ],
        out_specs=pl.BlockSpec((C, W), row_map(0)),
        out_shape=jax.ShapeDtypeStruct((batch * seq, W), BF16),
        scratch_shapes=[
            pltpu.VMEM((C + SUBLANES, 3 * W), F32),
            pltpu.VMEM((DN_HEADS, HEAD_DIM, HEAD_DIM), F32),
        ],
        compiler_params=pltpu.CompilerParams(
            dimension_semantics=("arbitrary", "arbitrary"),
            vmem_limit_bytes=VMEM_LIMIT),
        name="deltanet",
    )(proj, proj, proj, proj, small, conv_w, alog_row, dtb_row, dng_row)


def _merge_kernel(ya_ref, yb_ref, ga_ref, gb_ref, x_ref, wa_ref, wb_ref, wo_ref, fg_ref,
                  o_ref, *, final_norm):
    pa = jnp.dot(ya_ref[...], wa_ref[...], preferred_element_type=F32)
    pb = jnp.dot(yb_ref[...], wb_ref[...], preferred_element_type=F32)
    merged = ga_ref[...].astype(F32) * pa + gb_ref[...].astype(F32) * pb
    xn = x_ref[...] + jnp.dot(merged.astype(BF16), wo_ref[...], preferred_element_type=F32)
    if final_norm:
        xn = xn * lax.rsqrt(jnp.mean(xn * xn, axis=-1, keepdims=True) + NORM_EPS) * fg_ref[...]
    o_ref[...] = xn


def _merge(ya, yb, proj, x2, wa, wb, wo, fgain, *, tm, ga_col, gb_col, final_norm):
    m, d = x2.shape
    wa_n = ya.shape[1]
    wb_n = yb.shape[1]
    kern = functools.partial(_merge_kernel, final_norm=final_norm)
    const = lambda i: (0, 0)
    resident = functools.partial(pl.BlockSpec, index_map=const, pipeline_mode=pl.Buffered(1))
    return pl.pallas_call(
        kern,
        grid=(m // tm,),
        in_specs=[
            pl.BlockSpec((tm, wa_n), lambda i: (i, 0)),
            pl.BlockSpec((tm, wb_n), lambda i: (i, 0)),
            pl.BlockSpec((tm, d), lambda i: (i, ga_col)),
            pl.BlockSpec((tm, d), lambda i: (i, gb_col)),
            pl.BlockSpec((tm, d), lambda i: (i, 0)),
            resident((wa_n, d)),
            resident((wb_n, d)),
            resident((d, d)),
            pl.BlockSpec((1, d), const),
        ],
        out_specs=pl.BlockSpec((tm, d), lambda i: (i, 0)),
        out_shape=jax.ShapeDtypeStruct((m, d), F32),
        compiler_params=pltpu.CompilerParams(
            dimension_semantics=("arbitrary",),
            vmem_limit_bytes=VMEM_LIMIT),
        name="merge",
    )(ya, yb, proj, proj, x2, wa, wb, wo, fgain)


def _lane_row(vec, offset):
    return jnp.zeros((1, LANES), F32).at[0, offset:offset + vec.shape[0]].set(vec.astype(F32))


def kernel(x, norm_gain, w_in, conv_w, a_log, dt_bias, dn_norm_gain, w_branch_a, w_branch_b,
           w_out, final_norm_gain):
    batch, seq, d = x.shape
    depth = norm_gain.shape[0]
    mw = MOBA_HEADS * HEAD_DIM
    dw = DN_HEADS * HEAD_DIM
    assert mw == dw and d % mw == 0
    n_wide = 4 * mw + 4 * dw
    n_scalar = 2 * DN_HEADS
    tn = mw
    silu_tiles = (3, 7)
    x2 = x.reshape(batch * seq, d)
    for layer in range(depth):
        w = w_in[layer]
        w_act = w[:, :n_wide].astype(BF16)
        w_gate = w[:, n_wide + n_scalar:].astype(BF16)
        w_small = jnp.pad(w[:, n_wide:n_wide + n_scalar], ((0, 0), (0, LANES - n_scalar))).astype(BF16)
        proj, small = _inproj(x2, norm_gain[layer][None, :], w_act, w_gate, w_small,
                              tm=1024, tn=tn, silu_tiles=silu_tiles)
        ya = _moba(proj, batch=batch, seq=seq, q_col=0, k_col=MOBA_HEADS, v_col=2 * MOBA_HEADS,
                   z_col=3 * MOBA_HEADS)
        yb = _deltanet(proj, small, conv_w[layer],
                       _lane_row(a_log[layer], DN_HEADS), _lane_row(dt_bias[layer], DN_HEADS),
                       dn_norm_gain[layer][None, :].astype(F32),
                       batch=batch, seq=seq, q_col=4, k_col=5, v_col=6, z_col=7)
        x2 = _merge(ya, yb, proj, x2,
                    w_branch_a[layer].astype(BF16), w_branch_b[layer].astype(BF16),
                    w_out[layer].astype(BF16), final_norm_gain[None, :],
                    tm=256, ga_col=n_wide // d, gb_col=n_wide // d + 1,
                    final_norm=(layer == depth - 1))
    return x2.reshape(batch, seq, d)
```

```python
import functools
import math

import jax
import jax.numpy as jnp
from jax import lax
from jax.experimental import pallas as pl
from jax.experimental.pallas import tpu as pltpu

F32 = jnp.float32
BF16 = jnp.bfloat16

HEAD_DIM = 128
MOBA_HEADS = 8
MOBA_BLOCK = 256
MOBA_TOPK = 3
DN_HEADS = 8
DN_CONV = 4
DN_CHUNK = 64
NORM_EPS = 1e-6

LANES = 128
SUBLANES = 8
VMEM_LIMIT = 48 * 1024 * 1024
VMEM_LIMIT_BIG = 56 * 1024 * 1024

HI = lax.Precision.HIGHEST
NT_DIMS = (((1,), (1,)), ((), ()))


def _sigmoid(x):
    return 0.5 * (jnp.tanh(0.5 * x) + 1.0)


INPROJ_ROW_CHUNK = 256
INPROJ_COL_CHUNK = 256


def _inproj_kernel(x_ref, gain_ref, wa_ref, wg_ref, ws_ref, cw_ref, out_ref, small_ref,
                   h_scr, tail_scr, *, n_act_tiles, silu_tiles, conv_first, seq):
    i = pl.program_id(0)
    j = pl.program_id(1)
    tm = h_scr.shape[0]
    tn = out_ref.shape[1]
    rc, cc = INPROJ_ROW_CHUNK, INPROJ_COL_CHUNK
    row_chunks = [slice(r, r + rc) for r in range(0, tm, rc)]
    col_chunks = [slice(c, c + cc) for c in range(0, tn, cc)]
    is_conv = jnp.logical_and(j >= conv_first, j < conv_first + 3)

    @pl.when(jnp.logical_and(i == 0, j == 0))
    def _():
        tail_scr[...] = jnp.zeros_like(tail_scr)

    @pl.when(j == 0)
    def _():
        x = x_ref[...]
        inv = lax.rsqrt(jnp.mean(x * x, axis=-1, keepdims=True) + NORM_EPS)
        h = (x * inv * gain_ref[...]).astype(BF16)
        h_scr[...] = h
        small_ref[...] = jnp.dot(h, ws_ref[...], preferred_element_type=F32)

    def run_tile(w_ref, epilogue):
        pending = None
        for rows in row_chunks:
            for ci, cols in enumerate(col_chunks):
                acc = jnp.dot(h_scr[rows, :], w_ref[:, cols], preferred_element_type=F32)
                if pending is not None:
                    epilogue(*pending)
                pending = (acc, rows, ci, cols)
        epilogue(*pending)

    @pl.when(jnp.logical_and(j < n_act_tiles, jnp.logical_not(is_conv)))
    def _():
        is_silu = functools.reduce(jnp.logical_or, [j == t for t in silu_tiles])

        def epilogue(acc, rows, ci, cols):
            out_ref[rows, cols] = (acc * jnp.where(is_silu, _sigmoid(acc), 1.0)).astype(out_ref.dtype)

        run_tile(wa_ref, epilogue)

    def conv_tile(normalize):
        assert DN_CONV == 4
        part = j - conv_first
        cwh = 0.5 * cw_ref[...]
        hist_rows = SUBLANES
        hist = jnp.where(lax.rem(i * tm, seq) == 0, 0.0, tail_scr[part])
        hists = [hist[:, cols] for cols in col_chunks]
        norm_scale = jnp.where(part == 0, HEAD_DIM ** -0.5, 1.0)

        def epilogue(acc, rows, ci, cols):
            ext = jnp.concatenate([hists[ci], acc], axis=0)
            e1 = pltpu.roll(ext, 1, axis=0)
            lo = cwh[1:2, cols] * ext + cwh[0:1, cols] * e1
            hy = (cwh[3:4, cols] * ext + cwh[2:3, cols] * e1
                  + pltpu.roll(lo, 2, axis=0))[hist_rows:, :]
            y = hy + hy * jnp.tanh(hy)
            hists[ci] = acc[rc - hist_rows:, :]
            if normalize:
                heads = []
                for hd in range(cc // HEAD_DIM):
                    yh = y[:, hd * HEAD_DIM:(hd + 1) * HEAD_DIM]
                    inv = lax.rsqrt(jnp.sum(yh * yh, axis=-1, keepdims=True) + NORM_EPS)
                    heads.append(yh * (inv * norm_scale))
                y = jnp.concatenate(heads, axis=1)
            out_ref[rows, cols] = y.astype(out_ref.dtype)

        run_tile(wa_ref, epilogue)
        tail_scr[part] = jnp.concatenate(hists, axis=1)

    @pl.when(jnp.logical_and(is_conv, j < conv_first + 2))
    def _():
        conv_tile(True)

    @pl.when(j == conv_first + 2)
    def _():
        conv_tile(False)

    @pl.when(j >= n_act_tiles)
    def _():
        def epilogue(acc, rows, ci, cols):
            out_ref[rows, cols] = _sigmoid(acc).astype(out_ref.dtype)

        run_tile(wg_ref, epilogue)


def _inproj(x2, gain, w_act, w_gate, w_small, conv_w, *, tm, tn, silu_tiles, conv_first, seq):
    m, d = x2.shape
    assert seq % tm == 0 and conv_w.shape[1] == 3 * tn
    n_act_tiles = w_act.shape[1] // tn
    n_gate_tiles = w_gate.shape[1] // tn
    kern = functools.partial(_inproj_kernel, n_act_tiles=n_act_tiles, silu_tiles=silu_tiles,
                             conv_first=conv_first, seq=seq)
    return pl.pallas_call(
        kern,
        grid=(m // tm, n_act_tiles + n_gate_tiles),
        in_specs=[
            pl.BlockSpec((tm, d), lambda i, j: (i, 0)),
            pl.BlockSpec((1, d), lambda i, j: (0, 0)),
            pl.BlockSpec((d, tn), lambda i, j: (0, jnp.minimum(j, n_act_tiles - 1))),
            pl.BlockSpec((d, tn), lambda i, j: (0, jnp.maximum(j - n_act_tiles, 0))),
            pl.BlockSpec((d, LANES), lambda i, j: (0, 0)),
            pl.BlockSpec((DN_CONV, tn), lambda i, j: (0, jnp.clip(j - conv_first, 0, 2))),
        ],
        out_specs=[
            pl.BlockSpec((tm, tn), lambda i, j: (i, j)),
            pl.BlockSpec((tm, LANES), lambda i, j: (i, 0)),
        ],
        out_shape=[
            jax.ShapeDtypeStruct((m, (n_act_tiles + n_gate_tiles) * tn), BF16),
            jax.ShapeDtypeStruct((m, LANES), F32),
        ],
        scratch_shapes=[
            pltpu.VMEM((tm, d), BF16),
            pltpu.VMEM((3, SUBLANES, tn), F32),
        ],
        compiler_params=pltpu.CompilerParams(
            dimension_semantics=("arbitrary", "arbitrary"),
            vmem_limit_bytes=VMEM_LIMIT),
        name="inproj",
    )(x2, gain, w_act, w_gate, w_small, conv_w)


N_SPLIT = 3
LOG2E = math.log2(math.e)


def _split3(x):
    parts = []
    for _ in range(N_SPLIT):
        p = x.astype(BF16)
        parts.append(p)
        x = x - p.astype(F32)
    return parts


def _moba_kernel(q_ref, k_ref, v_ref, z_ref, o_ref, kx_scr, qx_scr, vt_scr, sel_scr, s_scr, *,
                 blk, n_blk, n_top, scale):
    h = pl.program_id(1)
    seq, dh = q_ref.shape
    neg_inf = jnp.float32(-jnp.inf)
    shift = int(math.log2(blk))
    assert 1 << shift == blk
    ln = lax.broadcasted_iota(jnp.int32, (seq, dh), 1)

    @pl.when(jnp.logical_and(pl.program_id(0) == 0, h == 0))
    def _():
        r = lax.broadcasted_iota(jnp.int32, (seq, dh), 0)
        kx_scr[:, dh:] = jnp.where(ln < N_SPLIT, r & (blk - 1), 0).astype(F32).astype(BF16)

    kx_scr[:, :dh] = k_ref[...]
    kmeans = []
    for n in range(n_blk):
        kmeans.append(jnp.mean(k_ref[n * blk:(n + 1) * blk, :].astype(F32), axis=0, keepdims=True))
        vt_scr[n] = v_ref[n * blk:(n + 1) * blk, :].astype(F32).T.astype(BF16)
    kmean = jnp.concatenate(kmeans, axis=0)

    slope2 = jnp.exp2(-(h + 1).astype(F32) * jnp.ones((1, 1), F32)) * LOG2E
    c_parts = _split3(slope2)
    q_aug = jnp.zeros((seq, dh), F32)
    for i, c in enumerate(c_parts):
        q_aug = jnp.where(ln == i, c.astype(F32), q_aug)
    q_raw = q_ref[...]
    qx_scr[:, :dh] = (q_raw.astype(F32) * (scale * LOG2E)).astype(BF16)
    qx_scr[:, dh:] = q_aug.astype(BF16)

    km_parts = jnp.concatenate(_split3(kmean), axis=0)
    gparts = lax.dot_general(km_parts, q_raw, NT_DIMS, preferred_element_type=F32)
    gate = functools.reduce(
        lambda a, b: a + b,
        [gparts[i * n_blk:(i + 1) * n_blk] for i in reversed(range(N_SPLIT))])
    row = lax.broadcasted_iota(jnp.int32, (n_blk, seq), 0)
    qblk = lax.broadcasted_iota(jnp.int32, (n_blk, seq), 1) >> shift
    past = row < qblk
    gm = jnp.where(past, gate, neg_inf)
    rank = jnp.zeros((n_blk, seq), jnp.int32)
    for m in range(n_blk):
        gm_m = gm[m:m + 1, :]
        beats = jnp.logical_or(gm_m > gm, jnp.logical_and(gm_m == gm, m < row))
        rank = rank + beats.astype(jnp.int32)
    sel = jnp.logical_and(past, rank < n_top)
    sel_scr[...] = jnp.where(sel, -(slope2 * blk) * (qblk - row).astype(F32), neg_inf)

    kk = lax.broadcasted_iota(jnp.int32, (blk, blk), 0)
    qq = lax.broadcasted_iota(jnp.int32, (blk, blk), 1)
    causal = kk <= qq

    def pass_a(slot, qi, n, m_run):
        s = lax.dot_general(kx_scr[n * blk:(n + 1) * blk, :], qx_scr[qi * blk:(qi + 1) * blk, :],
                            NT_DIMS, preferred_element_type=F32)
        if n == qi:
            s = jnp.where(causal, s, neg_inf)
        else:
            s = s + sel_scr[n:n + 1, qi * blk:(qi + 1) * blk]
        s_scr[slot, n] = s
        m = jnp.max(s, axis=0, keepdims=True)
        return m if m_run is None else jnp.maximum(m_run, m)

    def pass_b(slot, n, m_fin, state):
        p = jnp.exp2(s_scr[slot, n] - m_fin)
        l_n = jnp.sum(p, axis=0, keepdims=True)
        pv = jnp.dot(vt_scr[n], p.astype(BF16), preferred_element_type=F32)
        return (l_n, pv) if state is None else (state[0] + l_n, state[1] + pv)

    def finish(qi, state):
        l_i, acc = state
        out = (acc * (1.0 / l_i)).T
        rows = slice(qi * blk, (qi + 1) * blk)
        o_ref[rows, :] = (out * z_ref[rows, :].astype(F32)).astype(o_ref.dtype)

    def key_blocks(qi):
        return [qi] + list(range(qi))

    pending = None
    for idx, qi in enumerate(reversed(range(n_blk))):
        slot = idx % 2
        a_steps = key_blocks(qi)
        b_steps = key_blocks(pending[1]) if pending is not None else []
        m_run, state = None, None
        for t in range(max(len(a_steps), len(b_steps))):
            if t < len(a_steps):
                m_run = pass_a(slot, qi, a_steps[t], m_run)
            if t < len(b_steps):
                state = pass_b(pending[0], b_steps[t], pending[2], state)
        if pending is not None:
            finish(pending[1], state)
        pending = (slot, qi, m_run)
    state = None
    for n in key_blocks(pending[1]):
        state = pass_b(pending[0], n, pending[2], state)
    finish(pending[1], state)


def _moba(proj, *, batch, seq, q_col, k_col, v_col, z_col):
    blk = MOBA_BLOCK
    assert seq % blk == 0
    n_blk = seq // blk
    n_top = min(MOBA_TOPK, n_blk - 1)
    dh = HEAD_DIM
    kern = functools.partial(_moba_kernel, blk=blk, n_blk=n_blk, n_top=n_top, scale=dh ** -0.5)
    col_map = lambda col: (lambda b, h: (b, col + h))
    return pl.pallas_call(
        kern,
        grid=(batch, MOBA_HEADS),
        in_specs=[
            pl.BlockSpec((seq, dh), col_map(q_col)),
            pl.BlockSpec((seq, dh), col_map(k_col)),
            pl.BlockSpec((seq, dh), col_map(v_col)),
            pl.BlockSpec((seq, dh), col_map(z_col)),
        ],
        out_specs=pl.BlockSpec((seq, dh), col_map(0)),
        out_shape=jax.ShapeDtypeStruct((batch * seq, MOBA_HEADS * dh), BF16),
        scratch_shapes=[
            pltpu.VMEM((seq, 2 * dh), BF16),
            pltpu.VMEM((seq, 2 * dh), BF16),
            pltpu.VMEM((n_blk, dh, blk), BF16),
            pltpu.VMEM((n_blk, seq), F32),
            pltpu.VMEM((2, n_blk, blk, blk), F32),
        ],
        compiler_params=pltpu.CompilerParams(
            dimension_semantics=("arbitrary", "arbitrary"),
            vmem_limit_bytes=VMEM_LIMIT),
        name="moba",
    )(proj, proj, proj, proj)


def _split(x):
    hi = x.astype(BF16)
    return hi, (x - hi.astype(F32)).astype(BF16)


def _dot3(a_hi, a_lo, b_hi, b_lo):
    return jnp.dot(jnp.concatenate([a_hi, a_lo, a_hi], axis=1),
                   jnp.concatenate([b_hi, b_hi, b_lo], axis=0), preferred_element_type=F32)


def _deltanet_kernel(q_ref, k_ref, v_ref, z_ref, sm_ref, alog_ref, dtb_ref, dng_ref,
                     o_ref, st, *, n_heads, chunk):
    c = pl.program_id(1)
    C = chunk
    dk = HEAD_DIM
    neg_inf = jnp.float32(-jnp.inf)

    @pl.when(c == 0)
    def _():
        st[...] = jnp.zeros_like(st)

    sm = sm_ref[...]
    beta = _sigmoid(sm)
    a = sm + dtb_ref[...]
    softplus = jnp.maximum(a, 0.0) + jnp.log1p(jnp.exp(-jnp.abs(a)))
    g = -jnp.exp(alog_ref[...]) * softplus
    r_c = lax.broadcasted_iota(jnp.int32, (C, C), 0)
    c_c = lax.broadcasted_iota(jnp.int32, (C, C), 1)
    tri = (r_c >= c_c).astype(F32)
    gc = jnp.dot(tri, g, precision=HI, preferred_element_type=F32)
    gct = gc.T
    gl = gc[C - 1:C, :]
    eg = jnp.exp(gc)
    egl = jnp.exp(gl - gc)
    egl_last = jnp.exp(gl)

    R = lax.broadcasted_iota(jnp.int32, (2 * C, 2 * C), 0)
    Cc = lax.broadcasted_iota(jnp.int32, (2 * C, 2 * C), 1)
    same = (R >= C) == (Cc >= C)
    causal = jnp.logical_and(same, R >= Cc)
    strict = jnp.logical_and(same, R > Cc)
    lvl = jnp.where(strict, 31 - lax.clz(R ^ Cc), -1)
    eye = (R == Cc).astype(F32)
    dng = dng_ref[...]
    n_lvl = int(math.log2(C))

    def col(arr, la, lb):
        return jnp.concatenate([arr[:, la:la + 1], arr[:, lb:lb + 1]], axis=0)

    def stack(ref, ha, hb):
        return jnp.concatenate([ref[:, ha * dk:(ha + 1) * dk],
                                ref[:, hb * dk:(hb + 1) * dk]], axis=0)

    pairs = [(2 * p, 2 * p + 1) for p in range(n_heads // 2)]
    pre = []
    for ha, hb in pairs:
        beta_c = col(beta, ha, hb)
        gc_c = col(gc, n_heads + ha, n_heads + hb)
        eg_c = col(eg, n_heads + ha, n_heads + hb)
        egl_c = col(egl, n_heads + ha, n_heads + hb)
        gc_r = jnp.concatenate([gct[n_heads + ha:n_heads + ha + 1, :],
                                gct[n_heads + hb:n_heads + hb + 1, :]], axis=1)
        qn_b = stack(q_ref, ha, hb)
        kn_b = stack(k_ref, ha, hb)
        qn = qn_b.astype(F32)
        kn = kn_b.astype(F32)
        v2 = stack(v_ref, ha, hb).astype(F32)
        kb = kn * beta_c
        dec = jnp.exp(jnp.where(causal, gc_c - gc_r, neg_inf))
        kq = lax.dot_general(jnp.concatenate([kb.astype(BF16), qn_b], axis=0), kn_b,
                             NT_DIMS, preferred_element_type=F32)
        low = jnp.where(strict, kq[:2 * C] * dec, 0.0)
        intra = jnp.where(causal, kq[2 * C:] * dec, 0.0).astype(BF16)
        rhs = jnp.concatenate([v2 * beta_c, kb * eg_c], axis=1).astype(BF16)
        qe = (qn * eg_c).astype(BF16)
        kdt = (kn * egl_c).T.astype(BF16)
        pre.append((low, intra, rhs, qe, kdt))

    lows = [_split(p[0]) for p in pre]
    tms = [eye - jnp.where(lvl == 0, p[0], 0.0) for p in pre]
    for b in range(1, n_lvl):
        sel = lvl == b
        ts = [_split(t) for t in tms]
        yys = [_dot3(lh, ll, th, tl) for (lh, ll), (th, tl) in zip(lows, ts)]
        xs_ = [_dot3(th, tl, *_split(yy)) for (th, tl), yy in zip(ts, yys)]
        tms = [t - jnp.where(sel, x, 0.0) for t, x in zip(tms, xs_)]

    uws = [jnp.dot(t.astype(BF16), p[2], preferred_element_type=F32) for t, p in zip(tms, pre)]
    for (ha, hb), p, uw in zip(pairs, pre, uws):
        _, intra, _, qe, kdt = p
        u = uw[:, :dk]
        w_b = uw[:, dk:].astype(BF16)
        s_a = st[ha]
        s_b = st[hb]
        wq_a = jnp.dot(jnp.concatenate([w_b[:C], qe[:C]], axis=0), s_a.astype(BF16),
                       preferred_element_type=F32)
        wq_b = jnp.dot(jnp.concatenate([w_b[C:], qe[C:]], axis=0), s_b.astype(BF16),
                       preferred_element_type=F32)
        v_new = u - jnp.concatenate([wq_a[:C], wq_b[:C]], axis=0)
        vn_b = v_new.astype(BF16)
        out = (jnp.concatenate([wq_a[C:], wq_b[C:]], axis=0)
               + jnp.dot(intra, vn_b, preferred_element_type=F32))
        la, lb = n_heads + ha, n_heads + hb
        st[ha] = s_a * egl_last[:, la:la + 1] + jnp.dot(kdt[:, :C], vn_b[:C],
                                                         preferred_element_type=F32)
        st[hb] = s_b * egl_last[:, lb:lb + 1] + jnp.dot(kdt[:, C:], vn_b[C:],
                                                         preferred_element_type=F32)

        o = out * lax.rsqrt(jnp.mean(out * out, axis=-1, keepdims=True) + NORM_EPS) * dng
        o_ref[:, ha * dk:(ha + 1) * dk] = (
            o[:C] * z_ref[:, ha * dk:(ha + 1) * dk].astype(F32)).astype(o_ref.dtype)
        o_ref[:, hb * dk:(hb + 1) * dk] = (
            o[C:] * z_ref[:, hb * dk:(hb + 1) * dk].astype(F32)).astype(o_ref.dtype)


def _deltanet(proj, small, alog_row, dtb_row, dng_row, *, batch, seq,
              q_col, k_col, v_col, z_col):
    C = DN_CHUNK
    W = DN_HEADS * HEAD_DIM
    assert seq % C == 0
    n_chunk = seq // C
    kern = functools.partial(_deltanet_kernel, n_heads=DN_HEADS, chunk=C)
    row_map = lambda col: (lambda b, c: (b * n_chunk + c, col))
    const = lambda b, c: (0, 0)
    return pl.pallas_call(
        kern,
        grid=(batch, n_chunk),
        in_specs=[
            pl.BlockSpec((C, W), row_map(q_col)),
            pl.BlockSpec((C, W), row_map(k_col)),
            pl.BlockSpec((C, W), row_map(v_col)),
            pl.BlockSpec((C, W), row_map(z_col)),
            pl.BlockSpec((C, LANES), row_map(0)),
            pl.BlockSpec((1, LANES), const),
            pl.BlockSpec((1, LANES), const),
            pl.BlockSpec((1, HEAD_DIM), const),
        ],
        out_specs=pl.BlockSpec((C, W), row_map(0)),
        out_shape=jax.ShapeDtypeStruct((batch * seq, W), BF16),
        scratch_shapes=[pltpu.VMEM((DN_HEADS, HEAD_DIM, HEAD_DIM), F32)],
        compiler_params=pltpu.CompilerParams(
            dimension_semantics=("arbitrary", "arbitrary"),
            vmem_limit_bytes=VMEM_LIMIT),
        name="deltanet",
    )(proj, proj, proj, proj, small, alog_row, dtb_row, dng_row)


def _merge_kernel(ya_ref, yb_ref, ga_ref, gb_ref, x_ref, wa_ref, wb_ref, wo_ref, fg_ref,
                  o_ref, *, final_norm):
    pa = jnp.dot(ya_ref[...], wa_ref[...], preferred_element_type=F32)
    pb = jnp.dot(yb_ref[...], wb_ref[...], preferred_element_type=F32)
    merged = ga_ref[...].astype(F32) * pa + gb_ref[...].astype(F32) * pb
    xn = x_ref[...] + jnp.dot(merged.astype(BF16), wo_ref[...], preferred_element_type=F32)
    if final_norm:
        xn = xn * lax.rsqrt(jnp.mean(xn * xn, axis=-1, keepdims=True) + NORM_EPS) * fg_ref[...]
    o_ref[...] = xn


def _merge(ya, yb, proj, x2, wa, wb, wo, fgain, *, tm, ga_col, gb_col, final_norm):
    m, d = x2.shape
    wa_n = ya.shape[1]
    wb_n = yb.shape[1]
    kern = functools.partial(_merge_kernel, final_norm=final_norm)
    const = lambda i: (0, 0)
    resident = functools.partial(pl.BlockSpec, index_map=const, pipeline_mode=pl.Buffered(1))
    return pl.pallas_call(
        kern,
        grid=(m // tm,),
        in_specs=[
            pl.BlockSpec((tm, wa_n), lambda i: (i, 0)),
            pl.BlockSpec((tm, wb_n), lambda i: (i, 0)),
            pl.BlockSpec((tm, d), lambda i: (i, ga_col)),
            pl.BlockSpec((tm, d), lambda i: (i, gb_col)),
            pl.BlockSpec((tm, d), lambda i: (i, 0)),
            resident((wa_n, d)),
            resident((wb_n, d)),
            resident((d, d)),
            pl.BlockSpec((1, d), const),
        ],
        out_specs=pl.BlockSpec((tm, d), lambda i: (i, 0)),
        out_shape=jax.ShapeDtypeStruct((m, d), F32),
        compiler_params=pltpu.CompilerParams(
            dimension_semantics=("arbitrary",),
            vmem_limit_bytes=VMEM_LIMIT),
        name="merge",
    )(ya, yb, proj, proj, x2, wa, wb, wo, fgain)


def _lane_row(vec, offset):
    return jnp.zeros((1, LANES), F32).at[0, offset:offset + vec.shape[0]].set(vec.astype(F32))


def kernel(x, norm_gain, w_in, conv_w, a_log, dt_bias, dn_norm_gain, w_branch_a, w_branch_b,
           w_out, final_norm_gain):
    batch, seq, d = x.shape
    depth = norm_gain.shape[0]
    mw = MOBA_HEADS * HEAD_DIM
    dw = DN_HEADS * HEAD_DIM
    assert mw == dw and d % mw == 0
    n_wide = 4 * mw + 4 * dw
    n_scalar = 2 * DN_HEADS
    tn = mw
    silu_tiles = (3, 7)
    x2 = x.reshape(batch * seq, d)
    for layer in range(depth):
        w = w_in[layer]
        w_act = w[:, :n_wide].astype(BF16)
        w_gate = w[:, n_wide + n_scalar:].astype(BF16)
        w_small = jnp.pad(w[:, n_wide:n_wide + n_scalar], ((0, 0), (0, LANES - n_scalar))).astype(BF16)
        proj, small = _inproj(x2, norm_gain[layer][None, :], w_act, w_gate, w_small,
                              conv_w[layer].astype(F32), tm=1024, tn=tn, silu_tiles=silu_tiles,
                              conv_first=4, seq=seq)
        ya = _moba(proj, batch=batch, seq=seq, q_col=0, k_col=MOBA_HEADS, v_col=2 * MOBA_HEADS,
                   z_col=3 * MOBA_HEADS)
        yb = _deltanet(proj, small,
                       _lane_row(a_log[layer], DN_HEADS), _lane_row(dt_bias[layer], DN_HEADS),
                       dn_norm_gain[layer][None, :].astype(F32),
                       batch=batch, seq=seq, q_col=4, k_col=5, v_col=6, z_col=7)
        x2 = _merge(ya, yb, proj, x2,
                    w_branch_a[layer].astype(BF16), w_branch_b[layer].astype(BF16),
                    w_out[layer].astype(BF16), final_norm_gain[None, :],
                    tm=256, ga_col=n_wide // d, gb_col=n_wide // d + 1,
                    final_norm=(layer == depth - 1))
    return x2.reshape(batch, seq, d)
```
